```python
import math
import jax, jax.numpy as jnp
from jax import lax
import numpy as np

D_MODEL = 2048
BATCH = 4
SEQ = 2048
DEPTH = 4

N_MIXERS = 2
N_GLA = (DEPTH + N_MIXERS - 1) // N_MIXERS
N_NA = DEPTH // N_MIXERS

GRID_W = 64

GLA_HEADS = 4
GLA_DK = D_MODEL // 2
GLA_DV = D_MODEL
GLA_DK_HEAD = GLA_DK // GLA_HEADS
GLA_DV_HEAD = GLA_DV // GLA_HEADS
GLA_GATE_RANK = 16
GLA_GATE_TEMP = 16.0
GLA_CHUNK = 64

NA_HEADS = 16
NA_HEAD_DIM = D_MODEL // NA_HEADS
NA_MAX_KR = 8
NA_KC = 16
NA_NCB = GRID_W // NA_KC

D_FF = 11 * D_MODEL // 4
CONV_W = 3

LN_EPS = 1e-5
RMS_EPS = 1e-6
NEG_INF = -1e9
DEEPNORM_ALPHA = (2.0 * DEPTH) ** 0.25
DEEPNORM_BETA = (8.0 * DEPTH) ** -0.25

kernel_name = "hybrid_gla_natten_convffn_deepnorm"


def _layer_norm(x, g, b):
    xf = x.astype(jnp.float32)
    mu = jnp.mean(xf, axis=-1, keepdims=True)
    xc = xf - mu
    var = jnp.mean(xc * xc, axis=-1, keepdims=True)
    return (xc * lax.rsqrt(var + LN_EPS) * g.astype(jnp.float32) + b.astype(jnp.float32)).astype(x.dtype)


def _gla_direction(q, k, v, log_a, strict):
    B, H, T, dk = q.shape
    dv = v.shape[-1]
    C = GLA_CHUNK
    N = T // C
    q = q.reshape(B, H, N, C, dk)
    k = k.reshape(B, H, N, C, dk)
    v = v.reshape(B, H, N, C, dv)
    b = jnp.cumsum(log_a.reshape(B, H, N, C, dk), axis=3)
    b_last = b[:, :, :, -1:, :]
    q_dec = q * jnp.exp(b)
    k_inv = k * jnp.exp(-b)
    k_state = k * jnp.exp(b_last - b)
    chunk_decay = jnp.exp(b_last[:, :, :, 0, :])
    mask = jnp.tril(jnp.ones((C, C), jnp.float32), k=-1 if strict else 0)
    scores = jnp.einsum('bhncd,bhnsd->bhncs', q_dec, k_inv) * mask
    o_intra = jnp.einsum('bhncs,bhnse->bhnce', scores, v)

    def step(S, inp):
        qd, ks, vv, dl = inp
        o = jnp.einsum('bhcd,bhde->bhce', qd, S)
        S = S * dl[..., None] + jnp.einsum('bhcd,bhce->bhde', ks, vv)
        return S, o

    xs = (jnp.moveaxis(q_dec, 2, 0), jnp.moveaxis(k_state, 2, 0),
          jnp.moveaxis(v, 2, 0), jnp.moveaxis(chunk_decay, 2, 0))
    S0 = jnp.zeros((B, H, dk, dv), jnp.float32)
    _, o_inter = lax.scan(step, S0, xs)
    o = o_intra + jnp.moveaxis(o_inter, 0, 2)
    return o.reshape(B, H, T, dv)


def _gla_mixer(x, w_in, w_gate_up, b_gate, norm_g, w_out):
    B, T, _ = x.shape
    p = x @ w_in
    q, k, v, g, lr = jnp.split(p, [GLA_DK, 2 * GLA_DK, 2 * GLA_DK + GLA_DV, 2 * GLA_DK + 2 * GLA_DV], axis=-1)
    lr = lr.reshape(B, T, 2, GLA_GATE_RANK)
    gate_logit = jnp.einsum('btzr,zrk->btzk', lr, w_gate_up) + b_gate
    log_a = jax.nn.log_sigmoid(gate_logit.astype(jnp.float32)) / GLA_GATE_TEMP
    log_a = log_a.reshape(B, T, 2, GLA_HEADS, GLA_DK_HEAD).transpose(2, 0, 3, 1, 4)

    def heads(t, hd):
        return t.astype(jnp.float32).reshape(B, T, GLA_HEADS, hd).transpose(0, 2, 1, 3)

    qh = heads(q, GLA_DK_HEAD) * (GLA_DK_HEAD ** -0.5)
    kh = heads(k, GLA_DK_HEAD)
    vh = heads(v, GLA_DV_HEAD)
    o_fwd = _gla_direction(qh, kh, vh, log_a[0], strict=False)
    flip = lambda t: jnp.flip(t, axis=2)
    o_bwd = flip(_gla_direction(flip(qh), flip(kh), flip(vh), flip(log_a[1]), strict=True))
    o = o_fwd + o_bwd
    o = o * lax.rsqrt(jnp.mean(o * o, axis=-1, keepdims=True) + RMS_EPS) * norm_g.astype(jnp.float32)
    o = o.transpose(0, 2, 1, 3).reshape(B, T, GLA_DV).astype(x.dtype)
    return (o * jax.nn.silu(g)) @ w_out


def _na_mixer(x, w_in, rpb, w_out):
    B, T, _ = x.shape
    rows = T // GRID_W
    kr = min(NA_MAX_KR, rows)
    hd = NA_HEAD_DIM
    qkv = (x @ w_in).astype(jnp.float32).reshape(B, rows, GRID_W, 3, NA_HEADS, hd)
    qkv = qkv.transpose(3, 0, 4, 1, 2, 5)
    q = qkv[0] * (hd ** -0.5)
    k = qkv[1]
    v = qkv[2]

    qcols = np.arange(GRID_W).reshape(NA_NCB, NA_KC)
    blk_start = np.clip(np.arange(NA_NCB) * NA_KC - NA_KC // 2, 0, GRID_W - 2 * NA_KC)
    key_cols = blk_start[:, None] + np.arange(2 * NA_KC)[None, :]
    win_c = np.clip(qcols - NA_KC // 2, 0, GRID_W - NA_KC)
    kc = key_cols[:, None, :]
    col_valid = (kc >= win_c[..., None]) & (kc < win_c[..., None] + NA_KC)
    dc_idx = np.clip(kc - qcols[..., None], -(NA_KC - 1), NA_KC - 1) + NA_KC - 1
    row_start = np.clip(np.arange(rows) - kr // 2, 0, rows - kr)
    dr_idx = row_start[:, None] + np.arange(kr)[None, :] - np.arange(rows)[:, None] + NA_MAX_KR - 1

    bias = rpb.astype(jnp.float32)[:, dr_idx]
    bias = bias[..., dc_idx]
    bias = bias.transpose(1, 0, 3, 4, 2, 5)
    bias = jnp.where(col_valid[None, None, :, :, None, :], bias, NEG_INF)

    def row_block(inp):
        q_r, rs, bias_r = inp
        k_band = lax.dynamic_slice_in_dim(k, rs, kr, axis=2)
        v_band = lax.dynamic_slice_in_dim(v, rs, kr, axis=2)
        k_blk = k_band[:, :, :, key_cols]
        v_blk = v_band[:, :, :, key_cols]
        q_blk = q_r.reshape(B, NA_HEADS, NA_NCB, NA_KC, hd)
        s = jnp.einsum('bhjqd,bhrjkd->bhjqrk', q_blk, k_blk) + bias_r
        pr = jax.nn.softmax(s, axis=(-2, -1))
        return jnp.einsum('bhjqrk,bhrjkd->bhjqd', pr, v_blk)

    q_rows = q.transpose(2, 0, 1, 3, 4)
    out = lax.map(row_block, (q_rows, jnp.asarray(row_start, jnp.int32), bias))
    out = out.reshape(rows, B, NA_HEADS, GRID_W, hd).transpose(1, 0, 3, 2, 4).reshape(B, T, D_MODEL)
    return out.astype(x.dtype) @ w_out


def _conv_ffn(x, w_up, conv_w, conv_b, w_down):
    h = x @ w_up
    hp = jnp.pad(h, ((0, 0), (1, 1), (0, 0)))
    h = hp[:, :-2] * conv_w[0] + hp[:, 1:-1] * conv_w[1] + hp[:, 2:] * conv_w[2] + conv_b
    a, b = jnp.split(h, 2, axis=-1)
    return (jax.nn.gelu(a, approximate=False) * b) @ w_down


def setup_inputs(seed: int = 0) -> dict:
    key = jax.random.key(seed)
    ks = jax.random.split(key, 17)
    f32 = jnp.float32

    def nrm(k, shape, s):
        return jax.random.normal(k, shape, f32) * s

    x = nrm(ks[0], (BATCH, SEQ, D_MODEL), 1.0)

    gla_cols = 2 * GLA_DK + 2 * GLA_DV + 2 * GLA_GATE_RANK
    col = jnp.arange(gla_cols)
    gla_scale = jnp.where((col >= 2 * GLA_DK) & (col < 2 * GLA_DK + GLA_DV), DEEPNORM_BETA, 1.0).astype(f32)
    gla_w_in = nrm(ks[1], (N_GLA, D_MODEL, gla_cols), D_MODEL ** -0.5) * gla_scale
    gla_w_gate_up = nrm(ks[2], (N_GLA, 2, GLA_GATE_RANK, GLA_DK), GLA_GATE_RANK ** -0.5)
    gla_b_gate = nrm(ks[3], (N_GLA, 2, GLA_DK), 0.01)
    gla_norm_g = 1.0 + nrm(ks[4], (N_GLA, GLA_DV_HEAD), 0.02)
    gla_w_out = nrm(ks[5], (N_GLA, GLA_DV, D_MODEL), GLA_DV ** -0.5 * DEEPNORM_BETA)

    na_col = jnp.arange(3 * D_MODEL)
    na_scale = jnp.where(na_col >= 2 * D_MODEL, DEEPNORM_BETA, 1.0).astype(f32)
    na_w_in = nrm(ks[6], (N_NA, D_MODEL, 3 * D_MODEL), D_MODEL ** -0.5) * na_scale
    na_rpb = nrm(ks[7], (N_NA, NA_HEADS, 2 * NA_MAX_KR - 1, 2 * NA_KC - 1), 0.05)
    na_w_out = nrm(ks[8], (N_NA, D_MODEL, D_MODEL), D_MODEL ** -0.5 * DEEPNORM_BETA)

    ffn_w_up = nrm(ks[9], (DEPTH, D_MODEL, 2 * D_FF), D_MODEL ** -0.5)
    ffn_conv_w = nrm(ks[10], (DEPTH, CONV_W, 2 * D_FF), CONV_W ** -0.5)
    ffn_conv_b = nrm(ks[11], (DEPTH, 2 * D_FF), 0.01)
    ffn_w_down = nrm(ks[12], (DEPTH, D_FF, D_MODEL), D_FF ** -0.5 * DEEPNORM_BETA)

    ln_mix_g = 1.0 + nrm(ks[13], (DEPTH, D_MODEL), 0.02)
    ln_mix_b = nrm(ks[14], (DEPTH, D_MODEL), 0.02)
    ln_ffn_g = 1.0 + nrm(ks[15], (DEPTH, D_MODEL), 0.02)
    ln_ffn_b = nrm(ks[16], (DEPTH, D_MODEL), 0.02)

    return {"x": x,
            "gla_w_in": gla_w_in, "gla_w_gate_up": gla_w_gate_up, "gla_b_gate": gla_b_gate,
            "gla_norm_g": gla_norm_g, "gla_w_out": gla_w_out,
            "na_w_in": na_w_in, "na_rpb": na_rpb, "na_w_out": na_w_out,
            "ffn_w_up": ffn_w_up, "ffn_conv_w": ffn_conv_w, "ffn_conv_b": ffn_conv_b, "ffn_w_down": ffn_w_down,
            "ln_mix_g": ln_mix_g, "ln_mix_b": ln_mix_b, "ln_ffn_g": ln_ffn_g, "ln_ffn_b": ln_ffn_b}


def reference(x, gla_w_in, gla_w_gate_up, gla_b_gate, gla_norm_g, gla_w_out,
              na_w_in, na_rpb, na_w_out,
              ffn_w_up, ffn_conv_w, ffn_conv_b, ffn_w_down,
              ln_mix_g, ln_mix_b, ln_ffn_g, ln_ffn_b):
    for i in range(DEPTH):
        j = i // N_MIXERS
        if i % N_MIXERS == 0:
            m = _gla_mixer(x, gla_w_in[j], gla_w_gate_up[j], gla_b_gate[j], gla_norm_g[j], gla_w_out[j])
        else:
            m = _na_mixer(x, na_w_in[j], na_rpb[j], na_w_out[j])
        x = _layer_norm(DEEPNORM_ALPHA * x + m, ln_mix_g[i], ln_mix_b[i])
        f = _conv_ffn(x, ffn_w_up[i], ffn_conv_w[i], ffn_conv_b[i], ffn_w_down[i])
        x = _layer_norm(DEEPNORM_ALPHA * x + f, ln_ffn_g[i], ln_ffn_b[i])
    return x
```

```python
import functools
import math

import jax
import jax.numpy as jnp
import numpy as np
from jax import lax
from jax.experimental import pallas as pl
from jax.experimental.pallas import tpu as pltpu

F32 = jnp.float32
BF16 = jnp.bfloat16

D_MODEL = 2048
SEQ = 2048
DEPTH = 4
GRID_W = 64
GLA_HEADS = 4
GLA_DK = D_MODEL // 2
GLA_DV = D_MODEL
GLA_DK_HEAD = GLA_DK // GLA_HEADS
GLA_DV_HEAD = GLA_DV // GLA_HEADS
GLA_RANK = 16
GLA_TEMP = 16.0
GLA_CHUNK = 64
NA_HEADS = 16
NA_HD = D_MODEL // NA_HEADS
NA_KR = 8
NA_KC = 16
D_FF = 11 * D_MODEL // 4
LN_EPS = 1e-5
RMS_EPS = 1e-6
NEG_INF = -1e9
ALPHA = (2.0 * DEPTH) ** 0.25

V7X_VMEM_BYTES = 64 * 1024 * 1024
VMEM_LIMIT = V7X_VMEM_BYTES - 8 * 1024 * 1024
HALO = 16


def _params(*sem):
    return pltpu.CompilerParams(dimension_semantics=sem, vmem_limit_bytes=VMEM_LIMIT)


def _layer_norm(y, g, b):
    mu = jnp.mean(y, axis=-1, keepdims=True)
    yc = y - mu
    var = jnp.mean(yc * yc, axis=-1, keepdims=True)
    return yc * lax.rsqrt(var + LN_EPS) * g + b


def _mm_kernel(x_ref, w_ref, o_ref):
    x = x_ref[...].astype(BF16)
    o_ref[...] = jnp.dot(x, w_ref[...], preferred_element_type=F32).astype(o_ref.dtype)


def _matmul(x, w, out_dtype, tm, tn):
    m, k = x.shape
    n = w.shape[1]
    assert m % tm == 0 and n % tn == 0
    return pl.pallas_call(
        _mm_kernel,
        grid=(m // tm, n // tn),
        in_specs=[pl.BlockSpec((tm, k), lambda i, j: (i, 0)),
                  pl.BlockSpec((k, tn), lambda i, j: (0, j))],
        out_specs=pl.BlockSpec((tm, tn), lambda i, j: (i, j)),
        out_shape=jax.ShapeDtypeStruct((m, n), out_dtype),
        compiler_params=_params("parallel", "arbitrary"),
        name="in_proj",
    )(x, w)


def _mm_ln_kernel(a_ref, w_ref, x_ref, g_ref, b_ref, o_ref):
    m = jnp.dot(a_ref[...], w_ref[...], preferred_element_type=F32)
    y = ALPHA * x_ref[...] + m
    o_ref[...] = _layer_norm(y, g_ref[...], b_ref[...])


def _matmul_res_ln(a, w, x, g, b, tm):
    m, k = a.shape
    d = w.shape[1]
    return pl.pallas_call(
        _mm_ln_kernel,
        grid=(m // tm,),
        in_specs=[pl.BlockSpec((tm, k), lambda i: (i, 0)),
                  pl.BlockSpec((k, d), lambda i: (0, 0)),
                  pl.BlockSpec((tm, d), lambda i: (i, 0)),
                  pl.BlockSpec((1, d), lambda i: (0, 0)),
                  pl.BlockSpec((1, d), lambda i: (0, 0))],
        out_specs=pl.BlockSpec((tm, d), lambda i: (i, 0)),
        out_shape=jax.ShapeDtypeStruct((m, d), F32),
        compiler_params=_params("parallel"),
        name="out_proj_ln",
    )(a, w, x, g.reshape(1, d), b.reshape(1, d))


def _gelu(a):
    return 0.5 * a * (1.0 + lax.erf(a * (2.0 ** -0.5)))


def _ffn_kernel(xp_ref, x_ref, xn_ref, wa_ref, wb_ref, cwa_ref, cwb_ref, cba_ref, cbb_ref,
                wd_ref, g_ref, b_ref, o_ref, xe_ref, *, tm, tiles_per_seq):
    i = pl.program_id(0)
    j = pl.program_id(1)
    nj = pl.num_programs(1)
    rows = tm + 2 * HALO

    @pl.when(j == 0)
    def _stage_rows():
        pos = i % tiles_per_seq
        keep_prev = (pos != 0).astype(F32)
        keep_next = (pos != tiles_per_seq - 1).astype(F32)
        zeros = jnp.zeros((HALO - 8, x_ref.shape[1]), F32)
        prev = jnp.concatenate([zeros, xp_ref[...] * keep_prev], axis=0)
        nxt = jnp.concatenate([xn_ref[...] * keep_next, zeros], axis=0)
        xe_ref[0:HALO, :] = prev.astype(BF16)
        xe_ref[HALO:HALO + tm, :] = x_ref[...].astype(BF16)
        xe_ref[HALO + tm:rows, :] = nxt.astype(BF16)

    xe = xe_ref[...]

    def conv_half(w_ref, cw_ref, cb_ref):
        h = jnp.dot(xe, w_ref[...], preferred_element_type=F32)
        h_prev = pltpu.roll(h, 1, 0)[HALO:HALO + tm]
        h_next = pltpu.roll(h, rows - 1, 0)[HALO:HALO + tm]
        cw = cw_ref[...]
        return (h_prev * cw[0:1] + h[HALO:HALO + tm] * cw[1:2] + h_next * cw[2:3]
                + cb_ref[...])

    a = conv_half(wa_ref, cwa_ref, cba_ref)
    bb = conv_half(wb_ref, cwb_ref, cbb_ref)
    u = (_gelu(a) * bb).astype(BF16)
    contrib = jnp.dot(u, wd_ref[...], preferred_element_type=F32)

    @pl.when(j == 0)
    def _init():
        o_ref[...] = contrib

    @pl.when(j > 0)
    def _acc():
        o_ref[...] += contrib

    @pl.when(j == nj - 1)
    def _finish():
        y = ALPHA * x_ref[...] + o_ref[...]
        o_ref[...] = _layer_norm(y, g_ref[...], b_ref[...])


def _conv_ffn_ln(x, w_up, conv_w, conv_b, w_down, g, b, tm, tf):
    m, d = x.shape
    f = w_down.shape[0]
    nf = f // tf
    assert m % tm == 0 and f % tf == 0 and SEQ % tm == 0 and tm % 8 == 0
    tb = tm // 8
    last_blk = m // 8 - 1
    conv_b = conv_b.reshape(1, 2 * f)
    kern = functools.partial(_ffn_kernel, tm=tm, tiles_per_seq=SEQ // tm)
    return pl.pallas_call(
        kern,
        grid=(m // tm, nf),
        in_specs=[
            pl.BlockSpec((8, d), lambda i, j: (jnp.maximum(i * tb - 1, 0), 0)),
            pl.BlockSpec((tm, d), lambda i, j: (i, 0)),
            pl.BlockSpec((8, d), lambda i, j: (jnp.minimum((i + 1) * tb, last_blk), 0)),
            pl.BlockSpec((d, tf), lambda i, j: (0, j)),
            pl.BlockSpec((d, tf), lambda i, j: (0, j + nf)),
            pl.BlockSpec((3, tf), lambda i, j: (0, j)),
            pl.BlockSpec((3, tf), lambda i, j: (0, j + nf)),
            pl.BlockSpec((1, tf), lambda i, j: (0, j)),
            pl.BlockSpec((1, tf), lambda i, j: (0, j + nf)),
            pl.BlockSpec((tf, d), lambda i, j: (j, 0)),
            pl.BlockSpec((1, d), lambda i, j: (0, 0)),
            pl.BlockSpec((1, d), lambda i, j: (0, 0)),
        ],
        out_specs=pl.BlockSpec((tm, d), lambda i, j: (i, 0)),
        out_shape=jax.ShapeDtypeStruct((m, d), F32),
        scratch_shapes=[pltpu.VMEM((tm + 2 * HALO, d), BF16)],
        compiler_params=_params("parallel", "arbitrary"),
        name="conv_ffn_ln",
    )(x, x, x, w_up, w_up, conv_w, conv_w, conv_b, conv_b, w_down,
      g.reshape(1, d), b.reshape(1, d))


def _split_dot(tri, x):
    hi = x.astype(BF16)
    r1 = x - hi.astype(F32)
    mid = r1.astype(BF16)
    lo = (r1 - mid.astype(F32)).astype(BF16)
    t = tri.astype(BF16)
    return (jnp.dot(t, hi, preferred_element_type=F32)
            + jnp.dot(t, mid, preferred_element_type=F32)
            + jnp.dot(t, lo, preferred_element_type=F32))


def _log_sigmoid(x):
    return jnp.minimum(x, 0.0) - jnp.log1p(jnp.exp(-jnp.abs(x)))


def _gla_kernel(q_ref, k_ref, v_ref, g_ref, lr_ref, wg_ref, bg_ref, ng_ref, o_ref,
                la_ref, of_ref, ob_ref, sf_ref, sb_ref):
    t = q_ref.shape[0]
    c = GLA_CHUNK
    n_chunks = t // c
    scale = GLA_DK_HEAD ** -0.5

    lr = lr_ref[...]
    for z in range(2):
        logit = jnp.dot(lr[:, z * GLA_RANK:(z + 1) * GLA_RANK], wg_ref[z],
                        preferred_element_type=F32, precision=lax.Precision.HIGHEST)
        la_ref[z] = _log_sigmoid(logit + bg_ref[z]) * (1.0 / GLA_TEMP)

    row = lax.broadcasted_iota(jnp.int32, (c, c), 0)
    col = lax.broadcasted_iota(jnp.int32, (c, c), 1)
    tri = ((row >= col).astype(F32), (row <= col).astype(F32))
    mask = (row >= col, col > row)
    tot_row = (c - 1, 0)
    s_refs = (sf_ref, sb_ref)
    o_refs = (of_ref, ob_ref)
    sf_ref[...] = jnp.zeros_like(sf_ref)
    sb_ref[...] = jnp.zeros_like(sb_ref)

    def chunk(z, n):
        sl = pl.ds(pl.multiple_of(n * c, c), c)
        la = la_ref[z, sl, :]
        cum = _split_dot(tri[z], la)
        tot = cum[tot_row[z]:tot_row[z] + 1, :]
        q = q_ref[sl, :].astype(F32) * scale
        k = k_ref[sl, :].astype(F32)
        v = v_ref[sl, :]
        qd = (q * jnp.exp(cum)).astype(BF16)
        ki = (k * jnp.exp(-cum)).astype(BF16)
        ks = (k * jnp.exp(tot - cum)).astype(BF16)
        s = lax.dot_general(qd, ki, (((1,), (1,)), ((), ())), preferred_element_type=F32)
        s = jnp.where(mask[z], s, 0.0).astype(BF16)
        st = s_refs[z][...]
        o = jnp.dot(s, v, preferred_element_type=F32)
        o = o + lax.dot_general(qd, st.astype(BF16), (((1,), (1,)), ((), ())),
                                preferred_element_type=F32)
        o_refs[z][sl, :] = o
        upd = lax.dot_general(v, ks, (((0,), (0,)), ((), ())), preferred_element_type=F32)
        s_refs[z][...] = st * jnp.exp(tot) + upd

    def body(n, carry):
        chunk(0, n)
        chunk(1, n_chunks - 1 - n)
        return carry

    lax.fori_loop(0, n_chunks, body, 0)

    o = of_ref[...] + ob_ref[...]
    o = o * lax.rsqrt(jnp.mean(o * o, axis=-1, keepdims=True) + RMS_EPS) * ng_ref[...]
    gate = g_ref[...].astype(F32)
    o_ref[...] = (o * (gate * jax.nn.sigmoid(gate))).astype(o_ref.dtype)


def _gla_core(p, lr, w_gate, b_gate, norm_g, batch):
    m = p.shape[0]
    t = m // batch
    h, dk, dv = GLA_HEADS, GLA_DK_HEAD, GLA_DV_HEAD
    return pl.pallas_call(
        _gla_kernel,
        grid=(batch, h),
        in_specs=[
            pl.BlockSpec((t, dk), lambda b, i: (b, i)),
            pl.BlockSpec((t, dk), lambda b, i: (b, h + i)),
            pl.BlockSpec((t, dv), lambda b, i: (b, h + i)),
            pl.BlockSpec((t, dv), lambda b, i: (b, 2 * h + i)),
            pl.BlockSpec((t, 2 * GLA_RANK), lambda b, i: (b, 0)),
            pl.BlockSpec((2, GLA_RANK, dk), lambda b, i: (0, 0, i)),
            pl.BlockSpec((2, 1, dk), lambda b, i: (0, 0, i)),
            pl.BlockSpec((1, dv), lambda b, i: (0, 0)),
        ],
        out_specs=pl.BlockSpec((t, dv), lambda b, i: (b, i)),
        out_shape=jax.ShapeDtypeStruct((m, GLA_DV), BF16),
        scratch_shapes=[pltpu.VMEM((2, t, dk), F32),
                        pltpu.VMEM((t, dv), F32), pltpu.VMEM((t, dv), F32),
                        pltpu.VMEM((dv, dk), F32), pltpu.VMEM((dv, dk), F32)],
        compiler_params=_params("parallel", "arbitrary"),
        name="gla_core",
    )(p, p, p, p, lr, w_gate, b_gate.reshape(2, 1, GLA_DK), norm_g.reshape(1, dv))


def _na_bias_table(rpb):
    qc = np.arange(GRID_W)[:, None]
    kc = np.arange(GRID_W)[None, :]
    win = np.clip(qc - NA_KC // 2, 0, GRID_W - NA_KC)
    valid = (kc >= win) & (kc < win + NA_KC)
    dc = np.clip(kc - qc, -(NA_KC - 1), NA_KC - 1) + NA_KC - 1
    dr = np.arange(NA_KR)[:, None] + np.arange(NA_KR)[None, :]
    tab = rpb.astype(F32)[:, dr]
    tab = tab[..., dc]
    tab = jnp.where(valid[None, None, None], tab, NEG_INF)
    tab = tab.transpose(0, 1, 3, 2, 4)
    return tab.reshape(rpb.shape[0], NA_KR, GRID_W, NA_KR * GRID_W)


def _na_kernel(q_ref, k_ref, v_ref, bias_ref, o_ref):
    rows = q_ref.shape[0] // GRID_W
    band = NA_KR * GRID_W
    scale = NA_HD ** -0.5

    def body(r, carry):
        rs = jnp.clip(r - NA_KR // 2, 0, rows - NA_KR)
        d0 = rs - r + NA_KR - 1
        qs = pl.ds(pl.multiple_of(r * GRID_W, GRID_W), GRID_W)
        ks = pl.ds(pl.multiple_of(rs * GRID_W, GRID_W), band)
        s = lax.dot_general(q_ref[qs, :], k_ref[ks, :], (((1,), (1,)), ((), ())),
                            preferred_element_type=F32)
        s = s * scale + bias_ref[0, d0]
        p = jnp.exp(s - jnp.max(s, axis=-1, keepdims=True))
        l = jnp.sum(p, axis=-1, keepdims=True)
        o = jnp.dot(p.astype(BF16), v_ref[ks, :], preferred_element_type=F32)
        o_ref[qs, :] = (o / l).astype(o_ref.dtype)
        return carry

    lax.fori_loop(0, rows, body, 0)


def _na_core(qkv, bias, batch):
    m = qkv.shape[0]
    t = m // batch
    h, hd = NA_HEADS, NA_HD
    return pl.pallas_call(
        _na_kernel,
        grid=(batch, h),
        in_specs=[
            pl.BlockSpec((t, hd), lambda b, i: (b, i)),
            pl.BlockSpec((t, hd), lambda b, i: (b, h + i)),
            pl.BlockSpec((t, hd), lambda b, i: (b, 2 * h + i)),
            pl.BlockSpec((1, NA_KR, GRID_W, NA_KR * GRID_W), lambda b, i: (i, 0, 0, 0)),
        ],
        out_specs=pl.BlockSpec((t, hd), lambda b, i: (b, i)),
        out_shape=jax.ShapeDtypeStruct((m, D_MODEL), BF16),
        compiler_params=_params("parallel", "arbitrary"),
        name="na_core",
    )(qkv, qkv, qkv, bias)


def kernel(x, gla_w_in, gla_w_gate_up, gla_b_gate, gla_norm_g, gla_w_out, na_w_in, na_rpb, na_w_out, ffn_w_up, ffn_conv_w, ffn_conv_b, ffn_w_down, ln_mix_g, ln_mix_b, ln_ffn_g, ln_ffn_b):
    batch, seq, d = x.shape
    assert seq == SEQ and d == D_MODEL
    n_main = 2 * GLA_DK + 2 * GLA_DV
    xf = x.reshape(batch * seq, d)
    for i in range(DEPTH):
        j = i // 2
        if i % 2 == 0:
            w_in = gla_w_in[j]
            p = _matmul(xf, w_in[:, :n_main].astype(BF16), BF16, tm=1024, tn=1024)
            lr = _matmul(xf, w_in[:, n_main:].astype(BF16), F32, tm=1024, tn=2 * GLA_RANK)
            mix = _gla_core(p, lr, gla_w_gate_up[j], gla_b_gate[j], gla_norm_g[j], batch)
            w_out = gla_w_out[j]
        else:
            qkv = _matmul(xf, na_w_in[j].astype(BF16), BF16, tm=1024, tn=1024)
            mix = _na_core(qkv, _na_bias_table(na_rpb[j]), batch)
            w_out = na_w_out[j]
        xf = _matmul_res_ln(mix, w_out.astype(BF16), xf, ln_mix_g[i], ln_mix_b[i], tm=512)
        xf = _conv_ffn_ln(xf, ffn_w_up[i].astype(BF16), ffn_conv_w[i], ffn_conv_b[i],
                          ffn_w_down[i].astype(BF16), ln_ffn_g[i], ln_ffn_b[i], tm=512, tf=512)
    return xf.reshape(batch, seq, d)
```

```python
import functools
import math

import jax
import jax.numpy as jnp
import numpy as np
from jax import lax
from jax.experimental import pallas as pl
from jax.experimental.pallas import tpu as pltpu

F32 = jnp.float32
BF16 = jnp.bfloat16

D_MODEL = 2048
SEQ = 2048
DEPTH = 4
GRID_W = 64
GLA_HEADS = 4
GLA_DK = D_MODEL // 2
GLA_DV = D_MODEL
GLA_DK_HEAD = GLA_DK // GLA_HEADS
GLA_DV_HEAD = GLA_DV // GLA_HEADS
GLA_RANK = 16
GLA_TEMP = 16.0
GLA_CHUNK = 64
GLA_LOCAL_GROUP = 4
GLA_STATE_GROUP = 4
NA_HEADS = 16
NA_HD = D_MODEL // NA_HEADS
NA_KR = 8
NA_KC = 16
NA_ROW_GROUP = 8
D_FF = 11 * D_MODEL // 4
LN_EPS = 1e-5
RMS_EPS = 1e-6
NEG_INF = -1e9
ALPHA = (2.0 * DEPTH) ** 0.25

V7X_VMEM_BYTES = 64 * 1024 * 1024
VMEM_LIMIT = V7X_VMEM_BYTES - 8 * 1024 * 1024
FFN_SUB = 256
HALO = 16


def _params(*sem):
    return pltpu.CompilerParams(dimension_semantics=sem, vmem_limit_bytes=VMEM_LIMIT)


def _layer_norm(y, g, b):
    mu = jnp.mean(y, axis=-1, keepdims=True)
    yc = y - mu
    var = jnp.mean(yc * yc, axis=-1, keepdims=True)
    return yc * lax.rsqrt(var + LN_EPS) * g + b


def _mm_kernel(x_ref, w_ref, o_ref):
    x = x_ref[...].astype(BF16)
    o_ref[...] = jnp.dot(x, w_ref[...], preferred_element_type=F32).astype(o_ref.dtype)


def _matmul(x, w, out_dtype, tm, tn):
    m, k = x.shape
    n = w.shape[1]
    assert m % tm == 0 and n % tn == 0
    return pl.pallas_call(
        _mm_kernel,
        grid=(m // tm, n // tn),
        in_specs=[pl.BlockSpec((tm, k), lambda i, j: (i, 0)),
                  pl.BlockSpec((k, tn), lambda i, j: (0, j))],
        out_specs=pl.BlockSpec((tm, tn), lambda i, j: (i, j)),
        out_shape=jax.ShapeDtypeStruct((m, n), out_dtype),
        compiler_params=_params("parallel", "arbitrary"),
        name="in_proj",
    )(x, w)


def _mm_ln_kernel(a_ref, w_ref, x_ref, g_ref, b_ref, o_ref):
    m = jnp.dot(a_ref[...], w_ref[...], preferred_element_type=F32)
    y = ALPHA * x_ref[...] + m
    o_ref[...] = _layer_norm(y, g_ref[...], b_ref[...])


def _matmul_res_ln(a, w, x, g, b, tm):
    m, k = a.shape
    d = w.shape[1]
    return pl.pallas_call(
        _mm_ln_kernel,
        grid=(m // tm,),
        in_specs=[pl.BlockSpec((tm, k), lambda i: (i, 0)),
                  pl.BlockSpec((k, d), lambda i: (0, 0)),
                  pl.BlockSpec((tm, d), lambda i: (i, 0)),
                  pl.BlockSpec((1, d), lambda i: (0, 0)),
                  pl.BlockSpec((1, d), lambda i: (0, 0))],
        out_specs=pl.BlockSpec((tm, d), lambda i: (i, 0)),
        out_shape=jax.ShapeDtypeStruct((m, d), F32),
        compiler_params=_params("parallel"),
        name="out_proj_ln",
    )(a, w, x, g.reshape(1, d), b.reshape(1, d))


def _gelu(a):
    return 0.5 * a * (1.0 + lax.erf(a * (2.0 ** -0.5)))


def _ffn_kernel(xp_ref, x_ref, xn_ref, wa_ref, wb_ref, cwa_ref, cwb_ref, cba_ref, cbb_ref,
                wd_ref, g_ref, b_ref, o_ref, xe_ref, *, tm, tiles_per_seq):
    i = pl.program_id(0)
    j = pl.program_id(1)
    nj = pl.num_programs(1)
    rows = tm + 2 * HALO

    @pl.when(j == 0)
    def _stage_rows():
        pos = i % tiles_per_seq
        keep_prev = (pos != 0).astype(F32)
        keep_next = (pos != tiles_per_seq - 1).astype(F32)
        zeros = jnp.zeros((HALO - 8, x_ref.shape[1]), F32)
        prev = jnp.concatenate([zeros, xp_ref[...] * keep_prev], axis=0)
        nxt = jnp.concatenate([xn_ref[...] * keep_next, zeros], axis=0)
        xe_ref[0:HALO, :] = prev.astype(BF16)
        xe_ref[HALO:HALO + tm, :] = x_ref[...].astype(BF16)
        xe_ref[HALO + tm:rows, :] = nxt.astype(BF16)
        o_ref[...] = jnp.zeros_like(o_ref)

    xe = xe_ref[...]

    def conv_half(w_ref, cw_ref, cb_ref, cs):
        h = jnp.dot(xe, w_ref[:, cs], preferred_element_type=F32)
        h_prev = pltpu.roll(h, 1, 0)[HALO:HALO + tm]
        h_next = pltpu.roll(h, rows - 1, 0)[HALO:HALO + tm]
        cw = cw_ref[:, cs]
        return (h_prev * cw[0:1] + h[HALO:HALO + tm] * cw[1:2] + h_next * cw[2:3]
                + cb_ref[:, cs])

    contrib = None
    for c in range(wa_ref.shape[1] // FFN_SUB):
        cs = slice(c * FFN_SUB, (c + 1) * FFN_SUB)
        a = conv_half(wa_ref, cwa_ref, cba_ref, cs)
        bb = conv_half(wb_ref, cwb_ref, cbb_ref, cs)
        u = (_gelu(a) * bb).astype(BF16)
        part = jnp.dot(u, wd_ref[cs, :], preferred_element_type=F32)
        contrib = part if contrib is None else contrib + part
    o_ref[...] += contrib

    @pl.when(j == nj - 1)
    def _finish():
        y = ALPHA * x_ref[...] + o_ref[...]
        o_ref[...] = _layer_norm(y, g_ref[...], b_ref[...])


def _conv_ffn_ln(x, w_up, conv_w, conv_b, w_down, g, b, tm, tf):
    m, d = x.shape
    f = w_down.shape[0]
    nf = f // tf
    assert m % tm == 0 and f % tf == 0 and SEQ % tm == 0 and tm % 8 == 0
    tb = tm // 8
    last_blk = m // 8 - 1
    conv_b = conv_b.reshape(1, 2 * f)
    kern = functools.partial(_ffn_kernel, tm=tm, tiles_per_seq=SEQ // tm)
    return pl.pallas_call(
        kern,
        grid=(m // tm, nf),
        in_specs=[
            pl.BlockSpec((8, d), lambda i, j: (jnp.maximum(i * tb - 1, 0), 0)),
            pl.BlockSpec((tm, d), lambda i, j: (i, 0)),
            pl.BlockSpec((8, d), lambda i, j: (jnp.minimum((i + 1) * tb, last_blk), 0)),
            pl.BlockSpec((d, tf), lambda i, j: (0, j)),
            pl.BlockSpec((d, tf), lambda i, j: (0, j + nf)),
            pl.BlockSpec((3, tf), lambda i, j: (0, j)),
            pl.BlockSpec((3, tf), lambda i, j: (0, j + nf)),
            pl.BlockSpec((1, tf), lambda i, j: (0, j)),
            pl.BlockSpec((1, tf), lambda i, j: (0, j + nf)),
            pl.BlockSpec((tf, d), lambda i, j: (j, 0)),
            pl.BlockSpec((1, d), lambda i, j: (0, 0)),
            pl.BlockSpec((1, d), lambda i, j: (0, 0)),
        ],
        out_specs=pl.BlockSpec((tm, d), lambda i, j: (i, 0)),
        out_shape=jax.ShapeDtypeStruct((m, d), F32),
        scratch_shapes=[pltpu.VMEM((tm + 2 * HALO, d), BF16)],
        compiler_params=_params("parallel", "arbitrary"),
        name="conv_ffn_ln",
    )(x, x, x, w_up, w_up, conv_w, conv_w, conv_b, conv_b, w_down,
      g.reshape(1, d), b.reshape(1, d))


def _split_dot(tri, x):
    hi = x.astype(BF16)
    r1 = x - hi.astype(F32)
    mid = r1.astype(BF16)
    lo = (r1 - mid.astype(F32)).astype(BF16)
    t = tri.astype(BF16)
    return (jnp.dot(t, hi, preferred_element_type=F32)
            + jnp.dot(t, mid, preferred_element_type=F32)
            + jnp.dot(t, lo, preferred_element_type=F32))


def _log_sigmoid(x):
    return jnp.minimum(x, 0.0) - jnp.log(1.0 + jnp.exp(-jnp.abs(x)))


def _gla_kernel(q_ref, k_ref, v_ref, g_ref, lr_ref, wg_ref, bg_ref, ng_ref, o_ref,
                la_ref, qd_ref, ks_ref, dl_ref, of_ref, ob_ref, sf_ref, sb_ref):
    t = q_ref.shape[0]
    c = GLA_CHUNK
    n_chunks = t // c
    scale = GLA_DK_HEAD ** -0.5
    nt = (((1,), (1,)), ((), ()))
    tn = (((0,), (0,)), ((), ()))

    lr = lr_ref[...]
    for z in range(2):
        logit = jnp.dot(lr[:, z * GLA_RANK:(z + 1) * GLA_RANK].astype(BF16),
                        wg_ref[z].astype(BF16), preferred_element_type=F32)
        la_ref[z] = _log_sigmoid(logit + bg_ref[z]) * (1.0 / GLA_TEMP)

    row = lax.broadcasted_iota(jnp.int32, (c, c), 0)
    col = lax.broadcasted_iota(jnp.int32, (c, c), 1)
    tri = ((row >= col).astype(F32), (row <= col).astype(F32))
    mask = (row >= col, col > row)
    tot_row = (c - 1, 0)
    s_refs = (sf_ref, sb_ref)
    o_refs = (of_ref, ob_ref)

    def local_body(it, carry):
        work = [(z, it * GLA_LOCAL_GROUP + u) for u in range(GLA_LOCAL_GROUP) for z in range(2)]
        staged = []
        for z, n in work:
            sl = pl.ds(pl.multiple_of(n * c, c), c)
            cum = _split_dot(tri[z], la_ref[z, sl, :])
            staged.append((z, n, sl, cum))
        scored = []
        for z, n, sl, cum in staged:
            tot = cum[tot_row[z]:tot_row[z] + 1, :]
            q = q_ref[sl, :].astype(F32) * scale
            k = k_ref[sl, :].astype(F32)
            qd = (q * jnp.exp(cum)).astype(BF16)
            ki = (k * jnp.exp(-cum)).astype(BF16)
            qd_ref[z, sl, :] = qd
            ks_ref[z, sl, :] = (k * jnp.exp(tot - cum)).astype(BF16)
            dl_ref[z, pl.ds(n, 1), :] = jnp.exp(tot)
            s = lax.dot_general(qd, ki, nt, preferred_element_type=F32)
            scored.append((z, sl, s))
        for z, sl, s in scored:
            s = jnp.where(mask[z], s, 0.0).astype(BF16)
            o_refs[z][sl, :] = jnp.dot(s, v_ref[sl, :], preferred_element_type=F32)
        return carry

    lax.fori_loop(0, n_chunks // GLA_LOCAL_GROUP, local_body, 0)

    sf_ref[...] = jnp.zeros_like(sf_ref)
    sb_ref[...] = jnp.zeros_like(sb_ref)

    def state_body(it, carry):
        steps = []
        for z in range(2):
            for u in range(GLA_STATE_GROUP):
                i = it * GLA_STATE_GROUP + u
                n = i if z == 0 else n_chunks - 1 - i
                sl = pl.ds(pl.multiple_of(n * c, c), c)
                upd = lax.dot_general(v_ref[sl, :], ks_ref[z, sl, :], tn,
                                      preferred_element_type=F32)
                steps.append((z, n, sl, upd))
        st = [sf_ref[...], sb_ref[...]]
        for z, n, sl, upd in steps:
            o_refs[z][sl, :] += lax.dot_general(qd_ref[z, sl, :], st[z].astype(BF16), nt,
                                                preferred_element_type=F32)
            st[z] = st[z] * dl_ref[z, pl.ds(n, 1), :] + upd
        sf_ref[...] = st[0]
        sb_ref[...] = st[1]
        return carry

    lax.fori_loop(0, n_chunks // GLA_STATE_GROUP, state_body, 0)

    o = of_ref[...] + ob_ref[...]
    o = o * lax.rsqrt(jnp.mean(o * o, axis=-1, keepdims=True) + RMS_EPS) * ng_ref[...]
    gate = g_ref[...].astype(F32)
    o_ref[...] = (o * (gate * jax.nn.sigmoid(gate))).astype(o_ref.dtype)


def _gla_core(p, lr, w_gate, b_gate, norm_g, batch):
    m = p.shape[0]
    t = m // batch
    h, dk, dv = GLA_HEADS, GLA_DK_HEAD, GLA_DV_HEAD
    return pl.pallas_call(
        _gla_kernel,
        grid=(batch, h),
        in_specs=[
            pl.BlockSpec((t, dk), lambda b, i: (b, i)),
            pl.BlockSpec((t, dk), lambda b, i: (b, h + i)),
            pl.BlockSpec((t, dv), lambda b, i: (b, h + i)),
            pl.BlockSpec((t, dv), lambda b, i: (b, 2 * h + i)),
            pl.BlockSpec((t, 2 * GLA_RANK), lambda b, i: (b, 0)),
            pl.BlockSpec((2, GLA_RANK, dk), lambda b, i: (0, 0, i)),
            pl.BlockSpec((2, 1, dk), lambda b, i: (0, 0, i)),
            pl.BlockSpec((1, dv), lambda b, i: (0, 0)),
        ],
        out_specs=pl.BlockSpec((t, dv), lambda b, i: (b, i)),
        out_shape=jax.ShapeDtypeStruct((m, GLA_DV), BF16),
        scratch_shapes=[pltpu.VMEM((2, t, dk), F32),
                        pltpu.VMEM((2, t, dk), BF16), pltpu.VMEM((2, t, dk), BF16),
                        pltpu.VMEM((2, t // GLA_CHUNK, dk), F32),
                        pltpu.VMEM((t, dv), F32), pltpu.VMEM((t, dv), F32),
                        pltpu.VMEM((dv, dk), F32), pltpu.VMEM((dv, dk), F32)],
        compiler_params=_params("parallel", "arbitrary"),
        name="gla_core",
    )(p, p, p, p, lr, w_gate, b_gate.reshape(2, 1, GLA_DK), norm_g.reshape(1, dv))


def _na_bias_table(rpb):
    qc = np.arange(GRID_W)[:, None]
    kc = np.arange(GRID_W)[None, :]
    win = np.clip(qc - NA_KC // 2, 0, GRID_W - NA_KC)
    valid = (kc >= win) & (kc < win + NA_KC)
    dc = np.clip(kc - qc, -(NA_KC - 1), NA_KC - 1) + NA_KC - 1
    dr = np.arange(NA_KR)[:, None] + np.arange(NA_KR)[None, :]
    tab = rpb.astype(F32)[:, dr]
    tab = tab[..., dc]
    tab = jnp.where(valid[None, None, None], tab, NEG_INF)
    tab = tab.transpose(0, 1, 3, 2, 4)
    return tab.reshape(rpb.shape[0], NA_KR, GRID_W, NA_KR * GRID_W)


def _na_kernel(q_ref, k_ref, v_ref, bias_ref, o_ref):
    rows = q_ref.shape[0] // GRID_W
    band = NA_KR * GRID_W
    scale = NA_HD ** -0.5

    def body(it, carry):
        group = []
        for u in range(NA_ROW_GROUP):
            r = it * NA_ROW_GROUP + u
            rs = jnp.clip(r - NA_KR // 2, 0, rows - NA_KR)
            d0 = rs - r + NA_KR - 1
            qs = pl.ds(pl.multiple_of(r * GRID_W, GRID_W), GRID_W)
            ks = pl.ds(pl.multiple_of(rs * GRID_W, GRID_W), band)
            s = lax.dot_general(q_ref[qs, :], k_ref[ks, :], (((1,), (1,)), ((), ())),
                                preferred_element_type=F32)
            group.append((qs, ks, d0, s))
        probs = []
        for qs, ks, d0, s in group:
            s = s * scale + bias_ref[0, d0]
            p = jnp.exp(s - jnp.max(s, axis=-1, keepdims=True))
            l = jnp.sum(p, axis=-1, keepdims=True)
            probs.append((qs, ks, p.astype(BF16), l))
        for qs, ks, p, l in probs:
            o = jnp.dot(p, v_ref[ks, :], preferred_element_type=F32)
            o_ref[qs, :] = (o / l).astype(o_ref.dtype)
        return carry

    lax.fori_loop(0, rows // NA_ROW_GROUP, body, 0)


def _na_core(qkv, bias, batch):
    m = qkv.shape[0]
    t = m // batch
    h, hd = NA_HEADS, NA_HD
    return pl.pallas_call(
        _na_kernel,
        grid=(batch, h),
        in_specs=[
            pl.BlockSpec((t, hd), lambda b, i: (b, i)),
            pl.BlockSpec((t, hd), lambda b, i: (b, h + i)),
            pl.BlockSpec((t, hd), lambda b, i: (b, 2 * h + i)),
            pl.BlockSpec((1, NA_KR, GRID_W, NA_KR * GRID_W), lambda b, i: (i, 0, 0, 0)),
        ],
        out_specs=pl.BlockSpec((t, hd), lambda b, i: (b, i)),
        out_shape=jax.ShapeDtypeStruct((m, D_MODEL), BF16),
        compiler_params=_params("parallel", "arbitrary"),
        name="na_core",
    )(qkv, qkv, qkv, bias)


def kernel(x, gla_w_in, gla_w_gate_up, gla_b_gate, gla_norm_g, gla_w_out, na_w_in, na_rpb, na_w_out, ffn_w_up, ffn_conv_w, ffn_conv_b, ffn_w_down, ln_mix_g, ln_mix_b, ln_ffn_g, ln_ffn_b):
    batch, seq, d = x.shape
    assert seq == SEQ and d == D_MODEL
    n_main = 2 * GLA_DK + 2 * GLA_DV
    xf = x.reshape(batch * seq, d)
    for i in range(DEPTH):
        j = i // 2
        if i % 2 == 0:
            w_in = gla_w_in[j]
            p = _matmul(xf, w_in[:, :n_main].astype(BF16), BF16, tm=1024, tn=1024)
            lr = _matmul(xf, w_in[:, n_main:].astype(BF16), F32, tm=1024, tn=2 * GLA_RANK)
            mix = _gla_core(p, lr, gla_w_gate_up[j], gla_b_gate[j], gla_norm_g[j], batch)
            w_out = gla_w_out[j]
        else:
            qkv = _matmul(xf, na_w_in[j].astype(BF16), BF16, tm=1024, tn=1024)
            mix = _na_core(qkv, _na_bias_table(na_rpb[j]), batch)
            w_out = na_w_out[j]
        xf = _matmul_res_ln(mix, w_out.astype(BF16), xf, ln_mix_g[i], ln_mix_b[i], tm=512)
        xf = _conv_ffn_ln(xf, ffn_w_up[i].astype(BF16), ffn_conv_w[i], ffn_conv_b[i],
                          ffn_w_down[i].astype(BF16), ln_ffn_g[i], ln_ffn_b[i], tm=512, tf=512)
    return xf.reshape(batch, seq, d)
```

```python
import functools
import math

import jax
import jax.numpy as jnp
import numpy as np
from jax import lax
from jax.experimental import pallas as pl
from jax.experimental.pallas import tpu as pltpu

F32 = jnp.float32
BF16 = jnp.bfloat16

D_MODEL = 2048
SEQ = 2048
DEPTH = 4
GRID_W = 64
GLA_HEADS = 4
GLA_DK = D_MODEL // 2
GLA_DV = D_MODEL
GLA_DK_HEAD = GLA_DK // GLA_HEADS
GLA_DV_HEAD = GLA_DV // GLA_HEADS
GLA_RANK = 16
GLA_TEMP = 16.0
GLA_CHUNK = 64
GLA_GROUP = 4
GLA_STATE_GROUP = 2
NA_HEADS = 16
NA_HD = D_MODEL // NA_HEADS
NA_KR = 8
NA_KC = 16
NA_ROW_GROUP = 8
D_FF = 11 * D_MODEL // 4
LN_EPS = 1e-5
RMS_EPS = 1e-6
NEG_INF = -1e9
ALPHA = (2.0 * DEPTH) ** 0.25

V7X_VMEM_BYTES = 64 * 1024 * 1024
VMEM_LIMIT = V7X_VMEM_BYTES - 8 * 1024 * 1024
LN_ROW_SPLIT = 4
FFN_SUB = 256
HALO = 16


def _params(*sem):
    return pltpu.CompilerParams(dimension_semantics=sem, vmem_limit_bytes=VMEM_LIMIT)


def _layer_norm(y, g, b):
    mu = jnp.mean(y, axis=-1, keepdims=True)
    yc = y - mu
    var = jnp.mean(yc * yc, axis=-1, keepdims=True)
    return yc * lax.rsqrt(var + LN_EPS) * g + b


def _mm_kernel(x_ref, w_ref, o_ref):
    x = x_ref[...].astype(BF16)
    o_ref[...] = jnp.dot(x, w_ref[...], preferred_element_type=F32).astype(o_ref.dtype)


def _matmul(x, w, layer, n, out_dtype, tm, tn):
    m, k = x.shape
    assert m % tm == 0 and n % tn == 0 and n <= w.shape[2]
    return pl.pallas_call(
        _mm_kernel,
        grid=(m // tm, n // tn),
        in_specs=[pl.BlockSpec((tm, k), lambda i, j: (i, 0)),
                  pl.BlockSpec((None, k, tn), lambda i, j: (layer, 0, j))],
        out_specs=pl.BlockSpec((tm, tn), lambda i, j: (i, j)),
        out_shape=jax.ShapeDtypeStruct((m, n), out_dtype),
        compiler_params=_params("parallel", "arbitrary"),
        name="in_proj",
    )(x, w)


def _mm_ln_kernel(a_ref, w_ref, x_ref, g_ref, b_ref, o_ref):
    sub = a_ref.shape[0] // LN_ROW_SPLIT
    for r in range(LN_ROW_SPLIT):
        rs = slice(r * sub, (r + 1) * sub)
        m = jnp.dot(a_ref[rs, :], w_ref[...], preferred_element_type=F32)
        y = ALPHA * x_ref[rs, :] + m
        o_ref[rs, :] = _layer_norm(y, g_ref[...], b_ref[...])


def _matmul_res_ln(a, w, w_layer, x, g, b, ln_layer, tm):
    m, k = a.shape
    d = w.shape[2]
    return pl.pallas_call(
        _mm_ln_kernel,
        grid=(m // tm,),
        in_specs=[pl.BlockSpec((tm, k), lambda i: (i, 0)),
                  pl.BlockSpec((None, k, d), lambda i: (w_layer, 0, 0)),
                  pl.BlockSpec((tm, d), lambda i: (i, 0)),
                  pl.BlockSpec((None, 1, d), lambda i: (ln_layer, 0, 0)),
                  pl.BlockSpec((None, 1, d), lambda i: (ln_layer, 0, 0))],
        out_specs=pl.BlockSpec((tm, d), lambda i: (i, 0)),
        out_shape=jax.ShapeDtypeStruct((m, d), F32),
        compiler_params=_params("parallel"),
        name="out_proj_ln",
    )(a, w, x, g, b)


def _gelu(a):
    return 0.5 * a * (1.0 + lax.erf(a * (2.0 ** -0.5)))


def _ffn_kernel(xp_ref, x_ref, xn_ref, wa_ref, wb_ref, cwa_ref, cwb_ref, cba_ref, cbb_ref,
                wd_ref, g_ref, b_ref, o_ref, xe_ref, *, tm, tiles_per_seq):
    i = pl.program_id(0)
    j = pl.program_id(1)
    nj = pl.num_programs(1)
    rows = tm + 2 * HALO

    @pl.when(j == 0)
    def _stage_rows():
        pos = i % tiles_per_seq
        keep_prev = (pos != 0).astype(F32)
        keep_next = (pos != tiles_per_seq - 1).astype(F32)
        zeros = jnp.zeros((HALO - 8, x_ref.shape[1]), F32)
        prev = jnp.concatenate([zeros, xp_ref[...] * keep_prev], axis=0)
        nxt = jnp.concatenate([xn_ref[...] * keep_next, zeros], axis=0)
        xe_ref[0:HALO, :] = prev.astype(BF16)
        xe_ref[HALO:HALO + tm, :] = x_ref[...].astype(BF16)
        xe_ref[HALO + tm:rows, :] = nxt.astype(BF16)
        o_ref[...] = jnp.zeros_like(o_ref)

    xe = xe_ref[...]

    def conv_half(w_ref, cw_ref, cb_ref, cs):
        h = jnp.dot(xe, w_ref[:, cs], preferred_element_type=F32)
        h_prev = pltpu.roll(h, 1, 0)[HALO:HALO + tm]
        h_next = pltpu.roll(h, rows - 1, 0)[HALO:HALO + tm]
        cw = cw_ref[:, cs]
        return (h_prev * cw[0:1] + h[HALO:HALO + tm] * cw[1:2] + h_next * cw[2:3]
                + cb_ref[:, cs])

    contrib = None
    for c in range(wa_ref.shape[1] // FFN_SUB):
        cs = slice(c * FFN_SUB, (c + 1) * FFN_SUB)
        a = conv_half(wa_ref, cwa_ref, cba_ref, cs)
        bb = conv_half(wb_ref, cwb_ref, cbb_ref, cs)
        u = (_gelu(a) * bb).astype(BF16)
        part = jnp.dot(u, wd_ref[cs, :], preferred_element_type=F32)
        contrib = part if contrib is None else contrib + part
    o_ref[...] += contrib

    @pl.when(j == nj - 1)
    def _finish():
        y = ALPHA * x_ref[...] + o_ref[...]
        o_ref[...] = _layer_norm(y, g_ref[...], b_ref[...])


def _conv_ffn_ln(x, w_up, conv_w, conv_b, w_down, g, b, layer, tm, tf):
    m, d = x.shape
    f = w_down.shape[1]
    nf = f // tf
    assert m % tm == 0 and f % tf == 0 and SEQ % tm == 0 and tm % 8 == 0
    tb = tm // 8
    last_blk = m // 8 - 1
    kern = functools.partial(_ffn_kernel, tm=tm, tiles_per_seq=SEQ // tm)
    return pl.pallas_call(
        kern,
        grid=(m // tm, nf),
        in_specs=[
            pl.BlockSpec((8, d), lambda i, j: (jnp.maximum(i * tb - 1, 0), 0)),
            pl.BlockSpec((tm, d), lambda i, j: (i, 0), pipeline_mode=pl.Buffered(1)),
            pl.BlockSpec((8, d), lambda i, j: (jnp.minimum((i + 1) * tb, last_blk), 0)),
            pl.BlockSpec((None, d, tf), lambda i, j: (layer, 0, j)),
            pl.BlockSpec((None, d, tf), lambda i, j: (layer, 0, j + nf)),
            pl.BlockSpec((None, 3, tf), lambda i, j: (layer, 0, j)),
            pl.BlockSpec((None, 3, tf), lambda i, j: (layer, 0, j + nf)),
            pl.BlockSpec((None, 1, tf), lambda i, j: (layer, 0, j)),
            pl.BlockSpec((None, 1, tf), lambda i, j: (layer, 0, j + nf)),
            pl.BlockSpec((None, tf, d), lambda i, j: (layer, j, 0)),
            pl.BlockSpec((None, 1, d), lambda i, j: (layer, 0, 0)),
            pl.BlockSpec((None, 1, d), lambda i, j: (layer, 0, 0)),
        ],
        out_specs=pl.BlockSpec((tm, d), lambda i, j: (i, 0), pipeline_mode=pl.Buffered(1)),
        out_shape=jax.ShapeDtypeStruct((m, d), F32),
        scratch_shapes=[pltpu.VMEM((tm + 2 * HALO, d), BF16)],
        compiler_params=_params("parallel", "arbitrary"),
        name="conv_ffn_ln",
    )(x, x, x, w_up, w_up, conv_w, conv_w, conv_b, conv_b, w_down, g, b)


def _bf16_terms(x, n_terms):
    terms = []
    for _ in range(n_terms - 1):
        t = x.astype(BF16)
        terms.append(t)
        x = x - t.astype(F32)
    terms.append(x.astype(BF16))
    return terms


def _split_dot(tri, x):
    t = tri.astype(BF16)
    hi, lo = _bf16_terms(x, 2)
    return (jnp.dot(t, hi, preferred_element_type=F32)
            + jnp.dot(t, lo, preferred_element_type=F32))


def _log_sigmoid(x):
    return jnp.minimum(x, 0.0) - jnp.log(1.0 + jnp.exp(-jnp.abs(x)))


def _gla_kernel(q_ref, k_ref, v_ref, g_ref, lr_ref, wg_ref, bg_ref, ng_ref, o_ref,
                la_ref, qg_ref, kg_ref, dg_ref, of_ref, ob_ref, sf_ref, sb_ref):
    t = q_ref.shape[0]
    c = GLA_CHUNK
    gp = GLA_GROUP
    gr = gp * c
    n_groups = t // gr
    scale = GLA_DK_HEAD ** -0.5
    nt = (((1,), (1,)), ((), ()))
    tn = (((0,), (0,)), ((), ()))

    lr = lr_ref[...]
    for z in range(2):
        logit = jnp.dot(lr[:, z * GLA_RANK:(z + 1) * GLA_RANK].astype(BF16),
                        wg_ref[z].astype(BF16), preferred_element_type=F32)
        la_ref[z] = _log_sigmoid(logit + bg_ref[z]) * (1.0 / GLA_TEMP)

    row = lax.broadcasted_iota(jnp.int32, (c, c), 0)
    col = lax.broadcasted_iota(jnp.int32, (c, c), 1)
    tri = ((row >= col).astype(F32), (row <= col).astype(F32))
    tot_row = (c - 1, 0)
    qrow = lax.broadcasted_iota(jnp.int32, (c, gr), 0)
    klane = lax.broadcasted_iota(jnp.int32, (c, gr), 1)
    s_refs = (sf_ref, sb_ref)
    o_refs = (of_ref, ob_ref)

    def sum_tots(tots, lo, hi):
        acc = jnp.zeros_like(tots[0])
        for p in range(lo, hi):
            acc = acc + tots[p]
        return acc

    def chunk_slices(g):
        return [pl.ds(pl.multiple_of(g * gr + p * c, c), c) for p in range(gp)]

    def local_cums(z, g):
        return [_split_dot(tri[z], la_ref[z, sl, :]) for sl in chunk_slices(g)]

    def local_scores(z, g, cums):
        sls = chunk_slices(g)
        tots = [cum[tot_row[z]:tot_row[z] + 1, :] for cum in cums]
        qd, ki, ks = [], [], []
        for sl, cum, tot in zip(sls, cums, tots):
            q = q_ref[sl, :].astype(F32) * scale
            k = k_ref[sl, :].astype(F32)
            qd.append(q * jnp.exp(cum))
            ki.append((k * jnp.exp(-cum)).astype(BF16))
            ks.append(k * jnp.exp(tot - cum))
        if z == 0:
            before = [sum_tots(tots, 0, p) for p in range(gp)]
            after = [sum_tots(tots, p + 1, gp) for p in range(gp)]
        else:
            before = [sum_tots(tots, p + 1, gp) for p in range(gp)]
            after = [sum_tots(tots, 0, p) for p in range(gp)]
        for p in range(gp):
            qg_ref[z, sls[p], :] = (qd[p] * jnp.exp(before[p])).astype(BF16)
            kg_ref[z, sls[p], :] = (ks[p] * jnp.exp(after[p])).astype(BF16)
        dg_ref[z, pl.ds(g, 1), :] = jnp.exp(sum_tots(tots, 0, gp))
        scores = []
        for pq in range(gp):
            keys = []
            for pk in range(gp):
                seen = pk < pq if z == 0 else pk > pq
                if pk == pq:
                    keys.append(ki[pk])
                elif not seen:
                    keys.append(jnp.zeros_like(ki[pk]))
                else:
                    lo, hi = (pk + 1, pq) if z == 0 else (pq + 1, pk)
                    keys.append((ks[pk] * jnp.exp(sum_tots(tots, lo, hi))).astype(BF16))
            scores.append(lax.dot_general(qd[pq].astype(BF16), jnp.concatenate(keys, axis=0),
                                          nt, preferred_element_type=F32))
        return scores

    def local_out(z, g, scores):
        probs = []
        for pq, s in enumerate(scores):
            visible = (klane <= qrow + pq * c) if z == 0 else (klane > qrow + pq * c)
            probs.append(jnp.where(visible, s, 0.0).astype(BF16))
        rows = pl.ds(pl.multiple_of(g * gr, gr), gr)
        o_refs[z][rows, :] = jnp.dot(jnp.concatenate(probs, axis=0), v_ref[rows, :],
                                     preferred_element_type=F32)

    def local_body(g, carry):
        cums = [local_cums(z, g) for z in range(2)]
        scores = [local_scores(z, g, cums[z]) for z in range(2)]
        for z in range(2):
            local_out(z, g, scores[z])
        return carry

    lax.fori_loop(0, n_groups, local_body, 0)

    sf_ref[...] = jnp.zeros_like(sf_ref)
    sb_ref[...] = jnp.zeros_like(sb_ref)

    def state_body(it, carry):
        steps = []
        for z in range(2):
            for u in range(GLA_STATE_GROUP):
                i = it * GLA_STATE_GROUP + u
                g = i if z == 0 else n_groups - 1 - i
                rows = pl.ds(pl.multiple_of(g * gr, gr), gr)
                upd = lax.dot_general(v_ref[rows, :], kg_ref[z, rows, :], tn,
                                      preferred_element_type=F32)
                steps.append((z, g, rows, upd))
        st = [sf_ref[...], sb_ref[...]]
        for z, g, rows, upd in steps:
            o_refs[z][rows, :] += lax.dot_general(qg_ref[z, rows, :], st[z].astype(BF16), nt,
                                                  preferred_element_type=F32)
            st[z] = st[z] * dg_ref[z, pl.ds(g, 1), :] + upd
        sf_ref[...] = st[0]
        sb_ref[...] = st[1]
        return carry

    lax.fori_loop(0, n_groups // GLA_STATE_GROUP, state_body, 0)

    o = of_ref[...] + ob_ref[...]
    o = o * lax.rsqrt(jnp.mean(o * o, axis=-1, keepdims=True) + RMS_EPS) * ng_ref[...]
    gate = g_ref[...].astype(F32)
    o_ref[...] = (o * (gate * jax.nn.sigmoid(gate))).astype(o_ref.dtype)


def _gla_core(p, lr, w_gate, b_gate, norm_g, layer, batch):
    m = p.shape[0]
    t = m // batch
    h, dk, dv = GLA_HEADS, GLA_DK_HEAD, GLA_DV_HEAD
    return pl.pallas_call(
        _gla_kernel,
        grid=(batch, h),
        in_specs=[
            pl.BlockSpec((t, dk), lambda b, i: (b, i)),
            pl.BlockSpec((t, dk), lambda b, i: (b, h + i)),
            pl.BlockSpec((t, dv), lambda b, i: (b, h + i)),
            pl.BlockSpec((t, dv), lambda b, i: (b, 2 * h + i)),
            pl.BlockSpec((t, 2 * GLA_RANK), lambda b, i: (b, 0)),
            pl.BlockSpec((None, 2, GLA_RANK, dk), lambda b, i: (layer, 0, 0, i)),
            pl.BlockSpec((None, 2, 1, dk), lambda b, i: (layer, 0, 0, i)),
            pl.BlockSpec((None, 1, dv), lambda b, i: (layer, 0, 0)),
        ],
        out_specs=pl.BlockSpec((t, dv), lambda b, i: (b, i)),
        out_shape=jax.ShapeDtypeStruct((m, GLA_DV), BF16),
        scratch_shapes=[pltpu.VMEM((2, t, dk), F32),
                        pltpu.VMEM((2, t, dk), BF16), pltpu.VMEM((2, t, dk), BF16),
                        pltpu.VMEM((2, t // (GLA_CHUNK * GLA_GROUP), dk), F32),
                        pltpu.VMEM((t, dv), F32), pltpu.VMEM((t, dv), F32),
                        pltpu.VMEM((dv, dk), F32), pltpu.VMEM((dv, dk), F32)],
        compiler_params=_params("parallel", "arbitrary"),
        name="gla_core",
    )(p, p, p, p, lr, w_gate, b_gate, norm_g)


NA_DR = 2 * NA_KR
NA_DC = 2 * NA_KC - 1
NA_PAIR_W = 2 * GRID_W


def _na_col_geometry(shape):
    log_w = GRID_W.bit_length() - 1
    assert 1 << log_w == GRID_W
    col = lax.broadcasted_iota(jnp.int32, shape, 1)
    qc = col >> (log_w + 1)
    half = (col >> log_w) & 1
    kc = col & (GRID_W - 1)
    win = jnp.clip(qc - NA_KC // 2, 0, GRID_W - NA_KC)
    valid = (kc >= win) & (kc < win + NA_KC)
    src = half * NA_DC + (kc - qc + NA_KC - 1)
    return src, valid


def _na_bias_kernel(rpb2_ref, o_ref, sel_ref, valid_ref):
    n = o_ref.shape[1]

    @pl.when((pl.program_id(0) == 0) & (pl.program_id(1) == 0))
    def _build_selector():
        src, valid = _na_col_geometry((GRID_W, n))
        r = lax.broadcasted_iota(jnp.int32, (GRID_W, n), 0)
        sel_ref[...] = jnp.where(valid & (r == src), 1.0, 0.0).astype(BF16)
        _, valid_o = _na_col_geometry((NA_DR, n))
        valid_ref[...] = jnp.where(valid_o, 1.0, 0.0)

    sel = sel_ref[...]
    acc = jnp.zeros(o_ref.shape, F32)
    for term in _bf16_terms(rpb2_ref[...], 3):
        acc = acc + jnp.dot(term, sel, preferred_element_type=F32)
    o_ref[...] = jnp.where(valid_ref[...] > 0.5, acc, NEG_INF)


def _na_bias_tiles(rpb):
    nl, nh = rpb.shape[:2]
    assert rpb.shape[2] == NA_DR - 1 and rpb.shape[3] == NA_DC and 2 * NA_DC <= GRID_W
    rp = jnp.pad(rpb.astype(F32), ((0, 0), (0, 0), (0, 2), (0, 0)))
    pad = jnp.zeros((nl, nh, NA_DR, GRID_W - 2 * NA_DC), F32)
    rpb2 = jnp.concatenate([rp[:, :, :NA_DR], rp[:, :, 1:NA_DR + 1], pad], axis=-1)
    n = GRID_W * NA_PAIR_W
    tiles = pl.pallas_call(
        _na_bias_kernel,
        grid=(nl, nh),
        in_specs=[pl.BlockSpec((None, None, NA_DR, GRID_W), lambda l, h: (l, h, 0, 0))],
        out_specs=pl.BlockSpec((None, None, NA_DR, n), lambda l, h: (l, h, 0, 0)),
        out_shape=jax.ShapeDtypeStruct((nl, nh, NA_DR, n), F32),
        scratch_shapes=[pltpu.VMEM((GRID_W, n), BF16), pltpu.VMEM((NA_DR, n), F32)],
        compiler_params=_params("arbitrary", "arbitrary"),
        name="na_bias",
    )(rpb2)
    return tiles.reshape(nl, nh, NA_DR, GRID_W, NA_PAIR_W)


def _na_kernel(q_ref, k_ref, v_ref, bias_ref, o_ref):
    rows = q_ref.shape[0] // GRID_W
    band = NA_KR * GRID_W
    scale = NA_HD ** -0.5

    def body(it, carry):
        group = []
        for u in range(NA_ROW_GROUP):
            r = it * NA_ROW_GROUP + u
            rs = jnp.clip(r - NA_KR // 2, 0, rows - NA_KR)
            d0 = rs - r + NA_KR - 1
            qs = pl.ds(pl.multiple_of(r * GRID_W, GRID_W), GRID_W)
            ks = pl.ds(pl.multiple_of(rs * GRID_W, GRID_W), band)
            s = lax.dot_general(q_ref[qs, :], k_ref[ks, :], (((1,), (1,)), ((), ())),
                                preferred_element_type=F32)
            group.append((qs, ks, d0, s))
        probs = []
        for qs, ks, d0, s in group:
            bias = jnp.concatenate([bias_ref[d0 + 2 * p] for p in range(NA_KR // 2)], axis=1)
            s = s * scale + bias
            p = jnp.exp(s - jnp.max(s, axis=-1, keepdims=True))
            l = jnp.sum(p, axis=-1, keepdims=True)
            probs.append((qs, ks, p.astype(BF16), l))
        for qs, ks, p, l in probs:
            o = jnp.dot(p, v_ref[ks, :], preferred_element_type=F32)
            o_ref[qs, :] = (o / l).astype(o_ref.dtype)
        return carry

    lax.fori_loop(0, rows // NA_ROW_GROUP, body, 0)


def _na_core(qkv, bias, layer, batch):
    m = qkv.shape[0]
    t = m // batch
    h, hd = NA_HEADS, NA_HD
    return pl.pallas_call(
        _na_kernel,
        grid=(batch, h),
        in_specs=[
            pl.BlockSpec((t, hd), lambda b, i: (b, i)),
            pl.BlockSpec((t, hd), lambda b, i: (b, h + i)),
            pl.BlockSpec((t, hd), lambda b, i: (b, 2 * h + i)),
            pl.BlockSpec((None, None, NA_DR, GRID_W, NA_PAIR_W),
                         lambda b, i: (layer, i, 0, 0, 0)),
        ],
        out_specs=pl.BlockSpec((t, hd), lambda b, i: (b, i)),
        out_shape=jax.ShapeDtypeStruct((m, D_MODEL), BF16),
        compiler_params=_params("parallel", "arbitrary"),
        name="na_core",
    )(qkv, qkv, qkv, bias)


def kernel(x, gla_w_in, gla_w_gate_up, gla_b_gate, gla_norm_g, gla_w_out, na_w_in, na_rpb, na_w_out, ffn_w_up, ffn_conv_w, ffn_conv_b, ffn_w_down, ln_mix_g, ln_mix_b, ln_ffn_g, ln_ffn_b):
    batch, seq, d = x.shape
    assert seq == SEQ and d == D_MODEL
    n_main = 2 * GLA_DK + 2 * GLA_DV
    n_lr = 2 * GLA_RANK
    gla_w = gla_w_in.astype(BF16)
    gla_w_lr = gla_w_in[:, :, n_main:].astype(BF16)
    gla_wo = gla_w_out.astype(BF16)
    gla_bg = gla_b_gate.reshape(gla_b_gate.shape[0], 2, 1, GLA_DK)
    gla_ng = gla_norm_g.reshape(gla_norm_g.shape[0], 1, GLA_DV_HEAD)
    na_w = na_w_in.astype(BF16)
    na_wo = na_w_out.astype(BF16)
    na_bias = _na_bias_tiles(na_rpb)
    w_up = ffn_w_up.astype(BF16)
    w_down = ffn_w_down.astype(BF16)
    conv_b = ffn_conv_b.reshape(DEPTH, 1, 2 * D_FF)
    mix_g, mix_b = ln_mix_g.reshape(DEPTH, 1, d), ln_mix_b.reshape(DEPTH, 1, d)
    ffn_g, ffn_b = ln_ffn_g.reshape(DEPTH, 1, d), ln_ffn_b.reshape(DEPTH, 1, d)

    xf = x.reshape(batch * seq, d)
    for i in range(DEPTH):
        j = i // 2
        if i % 2 == 0:
            p = _matmul(xf, gla_w, j, n_main, BF16, tm=1024, tn=1024)
            lr = _matmul(xf, gla_w_lr, j, n_lr, F32, tm=1024, tn=n_lr)
            mix = _gla_core(p, lr, gla_w_gate_up, gla_bg, gla_ng, j, batch)
            w_out = gla_wo
        else:
            qkv = _matmul(xf, na_w, j, 3 * d, BF16, tm=1024, tn=1024)
            mix = _na_core(qkv, na_bias, j, batch)
            w_out = na_wo
        xf = _matmul_res_ln(mix, w_out, j, xf, mix_g, mix_b, i, tm=512)
        xf = _conv_ffn_ln(xf, w_up, ffn_conv_w, conv_b, w_down, ffn_g, ffn_b, i,
                          tm=1024, tf=512)
    return xf.reshape(batch, seq, d)
```

```python
import functools
import math

import jax
import jax.numpy as jnp
import numpy as np
from jax import lax
from jax.experimental import pallas as pl
from jax.experimental.pallas import tpu as pltpu

F32 = jnp.float32
BF16 = jnp.bfloat16

D_MODEL = 2048
SEQ = 2048
DEPTH = 4
GRID_W = 64
GLA_HEADS = 4
GLA_DK = D_MODEL // 2
GLA_DV = D_MODEL
GLA_DK_HEAD = GLA_DK // GLA_HEADS
GLA_DV_HEAD = GLA_DV // GLA_HEADS
GLA_RANK = 16
GLA_TEMP = 16.0
GLA_CHUNK = 64
GLA_GROUP = 4
GLA_STATE_GROUP = 2
NA_HEADS = 16
NA_HD = D_MODEL // NA_HEADS
NA_KR = 8
NA_KC = 16
NA_ROW_GROUP = 8
D_FF = 11 * D_MODEL // 4
LN_EPS = 1e-5
RMS_EPS = 1e-6
NEG_INF = -1e9
ALPHA = (2.0 * DEPTH) ** 0.25

V7X_VMEM_BYTES = 64 * 1024 * 1024
VMEM_LIMIT = V7X_VMEM_BYTES - 8 * 1024 * 1024
LN_ROW_SPLIT = 4
PROJ_TM = 1024
PROJ_TN = 2048
FFN_TM = 1024
FFN_TF = 512
FFN_SUB = 256
HALO = 16


def _params(*sem):
    return pltpu.CompilerParams(dimension_semantics=sem, vmem_limit_bytes=VMEM_LIMIT)


def _layer_norm(y, g, b):
    mu = jnp.mean(y, axis=-1, keepdims=True)
    yc = y - mu
    var = jnp.mean(yc * yc, axis=-1, keepdims=True)
    return yc * lax.rsqrt(var + LN_EPS) * g + b


def _mm_kernel(x_ref, w_ref, o_ref):
    x = x_ref[...].astype(BF16)
    o_ref[...] = jnp.dot(x, w_ref[...], preferred_element_type=F32).astype(o_ref.dtype)


def _mm_side_kernel(x_ref, w_ref, ws_ref, o_ref, os_ref):
    x = x_ref[...].astype(BF16)
    o_ref[...] = jnp.dot(x, w_ref[...], preferred_element_type=F32).astype(o_ref.dtype)

    @pl.when(pl.program_id(1) == 0)
    def _side():
        os_ref[...] = jnp.dot(x, ws_ref[...], preferred_element_type=F32)


def _matmul(x, w, layer, n, out_dtype, tm, tn, w_side=None):
    m, k = x.shape
    assert m % tm == 0 and n % tn == 0 and n <= w.shape[2]
    in_specs = [pl.BlockSpec((tm, k), lambda i, j: (i, 0)),
                pl.BlockSpec((None, k, tn), lambda i, j: (layer, 0, j))]
    out_specs = pl.BlockSpec((tm, tn), lambda i, j: (i, j))
    out_shape = jax.ShapeDtypeStruct((m, n), out_dtype)
    if w_side is None:
        body, operands = _mm_kernel, (x, w)
    else:
        s = w_side.shape[2]
        body, operands = _mm_side_kernel, (x, w, w_side)
        in_specs.append(pl.BlockSpec((None, k, s), lambda i, j: (layer, 0, 0)))
        out_specs = (out_specs, pl.BlockSpec((tm, s), lambda i, j: (i, 0)))
        out_shape = (out_shape, jax.ShapeDtypeStruct((m, s), F32))
    return pl.pallas_call(
        body,
        grid=(m // tm, n // tn),
        in_specs=in_specs,
        out_specs=out_specs,
        out_shape=out_shape,
        compiler_params=_params("parallel", "arbitrary"),
        name="in_proj",
    )(*operands)


def _mm_ln_kernel(a_ref, w_ref, x_ref, g_ref, b_ref, o_ref):
    sub = a_ref.shape[0] // LN_ROW_SPLIT
    for r in range(LN_ROW_SPLIT):
        rs = slice(r * sub, (r + 1) * sub)
        m = jnp.dot(a_ref[rs, :], w_ref[...], preferred_element_type=F32)
        y = ALPHA * x_ref[rs, :] + m
        o_ref[rs, :] = _layer_norm(y, g_ref[...], b_ref[...])


def _matmul_res_ln(a, w, w_layer, x, g, b, ln_layer, tm):
    m, k = a.shape
    d = w.shape[2]
    return pl.pallas_call(
        _mm_ln_kernel,
        grid=(m // tm,),
        in_specs=[pl.BlockSpec((tm, k), lambda i: (i, 0)),
                  pl.BlockSpec((None, k, d), lambda i: (w_layer, 0, 0)),
                  pl.BlockSpec((tm, d), lambda i: (i, 0)),
                  pl.BlockSpec((None, 1, d), lambda i: (ln_layer, 0, 0)),
                  pl.BlockSpec((None, 1, d), lambda i: (ln_layer, 0, 0))],
        out_specs=pl.BlockSpec((tm, d), lambda i: (i, 0)),
        out_shape=jax.ShapeDtypeStruct((m, d), F32),
        compiler_params=_params("parallel"),
        name="out_proj_ln",
    )(a, w, x, g, b)


def _gelu(a):
    return 0.5 * a * (1.0 + lax.erf(a * (2.0 ** -0.5)))


def _ffn_kernel(xp_ref, x_ref, xn_ref, wup_ref, cw_ref, cb_ref, wd_ref, g_ref, b_ref,
                o_ref, xe_ref, *, tm, tiles_per_seq):
    i = pl.program_id(0)
    j = pl.program_id(1)
    nj = pl.num_programs(1)
    rows = tm + 2 * HALO
    tf = wd_ref.shape[0]
    d_ff = cw_ref.shape[1] // 2

    @pl.when(j == 0)
    def _stage_rows():
        pos = i % tiles_per_seq
        keep_prev = (pos != 0).astype(F32)
        keep_next = (pos != tiles_per_seq - 1).astype(F32)
        zeros = jnp.zeros((HALO - 8, x_ref.shape[1]), F32)
        prev = jnp.concatenate([zeros, xp_ref[...] * keep_prev], axis=0)
        nxt = jnp.concatenate([xn_ref[...] * keep_next, zeros], axis=0)
        xe_ref[0:HALO, :] = prev.astype(BF16)
        xe_ref[HALO:HALO + tm, :] = x_ref[...].astype(BF16)
        xe_ref[HALO + tm:rows, :] = nxt.astype(BF16)
        o_ref[...] = ALPHA * x_ref[...]

    xe = xe_ref[...]

    def conv_half(w_col, p_col):
        h = jnp.dot(xe, wup_ref[:, w_col:w_col + FFN_SUB], preferred_element_type=F32)
        h_prev = pltpu.roll(h, 1, 0)[HALO:HALO + tm]
        h_next = pltpu.roll(h, rows - 1, 0)[HALO:HALO + tm]
        ps = pl.ds(pl.multiple_of(p_col, FFN_SUB), FFN_SUB)
        cw = cw_ref[:, ps]
        return (h_prev * cw[0:1] + h[HALO:HALO + tm] * cw[1:2] + h_next * cw[2:3]
                + cb_ref[:, ps])

    contrib = None
    for c in range(tf // FFN_SUB):
        col = c * FFN_SUB
        a = conv_half(col, j * tf + col)
        bb = conv_half(tf + col, d_ff + j * tf + col)
        u = (_gelu(a) * bb).astype(BF16)
        part = jnp.dot(u, wd_ref[col:col + FFN_SUB, :], preferred_element_type=F32)
        contrib = part if contrib is None else contrib + part
    o_ref[...] += contrib

    @pl.when(j == nj - 1)
    def _finish():
        o_ref[...] = _layer_norm(o_ref[...], g_ref[...], b_ref[...])


def _conv_ffn_ln(x, w_up, conv_w, conv_b, w_down, g, b, layer, tm, tf):
    m, d = x.shape
    f = w_down.shape[1]
    nf = f // tf
    assert m % tm == 0 and f % tf == 0 and SEQ % tm == 0 and tm % 8 == 0
    assert w_up.shape[1:] == (nf, d, 2 * tf) and tf % FFN_SUB == 0
    tb = tm // 8
    last_blk = m // 8 - 1
    kern = functools.partial(_ffn_kernel, tm=tm, tiles_per_seq=SEQ // tm)
    return pl.pallas_call(
        kern,
        grid=(m // tm, nf),
        in_specs=[
            pl.BlockSpec((8, d), lambda i, j: (jnp.maximum(i * tb - 1, 0), 0)),
            pl.BlockSpec((tm, d), lambda i, j: (i, 0), pipeline_mode=pl.Buffered(1)),
            pl.BlockSpec((8, d), lambda i, j: (jnp.minimum((i + 1) * tb, last_blk), 0)),
            pl.BlockSpec((None, None, d, 2 * tf), lambda i, j: (layer, j, 0, 0)),
            pl.BlockSpec((None, 3, 2 * f), lambda i, j: (layer, 0, 0)),
            pl.BlockSpec((None, 1, 2 * f), lambda i, j: (layer, 0, 0)),
            pl.BlockSpec((None, tf, d), lambda i, j: (layer, j, 0)),
            pl.BlockSpec((None, 1, d), lambda i, j: (layer, 0, 0)),
            pl.BlockSpec((None, 1, d), lambda i, j: (layer, 0, 0)),
        ],
        out_specs=pl.BlockSpec((tm, d), lambda i, j: (i, 0), pipeline_mode=pl.Buffered(1)),
        out_shape=jax.ShapeDtypeStruct((m, d), F32),
        scratch_shapes=[pltpu.VMEM((tm + 2 * HALO, d), BF16)],
        compiler_params=_params("parallel", "arbitrary"),
        name="conv_ffn_ln",
    )(x, x, x, w_up, conv_w, conv_b, w_down, g, b)


def _tile_major_up(w_up, tf):
    nl, d, f2 = w_up.shape
    nf = f2 // 2 // tf
    w = w_up.reshape(nl, d, 2, nf, tf).transpose(0, 3, 1, 2, 4)
    return w.reshape(nl, nf, d, 2 * tf).astype(BF16)


def _bf16_terms(x, n_terms):
    terms = []
    for _ in range(n_terms - 1):
        t = x.astype(BF16)
        terms.append(t)
        x = x - t.astype(F32)
    terms.append(x.astype(BF16))
    return terms


def _split_dot(tri, x):
    t = tri.astype(BF16)
    hi, lo = _bf16_terms(x, 2)
    return (jnp.dot(t, hi, preferred_element_type=F32)
            + jnp.dot(t, lo, preferred_element_type=F32))


def _log_sigmoid(x):
    return jnp.minimum(x, 0.0) - jnp.log(1.0 + jnp.exp(-jnp.abs(x)))


def _gla_kernel(q_ref, k_ref, v_ref, g_ref, lr_ref, wg_ref, bg_ref, ng_ref, o_ref,
                la_ref, qg_ref, kg_ref, dg_ref, of_ref, ob_ref, sf_ref, sb_ref):
    t = q_ref.shape[0]
    c = GLA_CHUNK
    gp = GLA_GROUP
    gr = gp * c
    n_groups = t // gr
    scale = GLA_DK_HEAD ** -0.5
    nt = (((1,), (1,)), ((), ()))
    tn = (((0,), (0,)), ((), ()))

    lr = lr_ref[...]
    for z in range(2):
        logit = jnp.dot(lr[:, z * GLA_RANK:(z + 1) * GLA_RANK].astype(BF16),
                        wg_ref[z].astype(BF16), preferred_element_type=F32)
        la_ref[z] = _log_sigmoid(logit + bg_ref[z]) * (1.0 / GLA_TEMP)

    row = lax.broadcasted_iota(jnp.int32, (c, c), 0)
    col = lax.broadcasted_iota(jnp.int32, (c, c), 1)
    tri = ((row >= col).astype(F32), (row <= col).astype(F32))
    tot_row = (c - 1, 0)
    qrow = lax.broadcasted_iota(jnp.int32, (c, gr), 0)
    klane = lax.broadcasted_iota(jnp.int32, (c, gr), 1)
    s_refs = (sf_ref, sb_ref)
    o_refs = (of_ref, ob_ref)

    def sum_tots(tots, lo, hi):
        acc = jnp.zeros_like(tots[0])
        for p in range(lo, hi):
            acc = acc + tots[p]
        return acc

    def chunk_slices(g):
        return [pl.ds(pl.multiple_of(g * gr + p * c, c), c) for p in range(gp)]

    def local_cums(z, g):
        return [_split_dot(tri[z], la_ref[z, sl, :]) for sl in chunk_slices(g)]

    def local_scores(z, g, cums):
        sls = chunk_slices(g)
        tots = [cum[tot_row[z]:tot_row[z] + 1, :] for cum in cums]
        qd, ki, ks = [], [], []
        for sl, cum, tot in zip(sls, cums, tots):
            q = q_ref[sl, :].astype(F32) * scale
            k = k_ref[sl, :].astype(F32)
            qd.append(q * jnp.exp(cum))
            ki.append((k * jnp.exp(-cum)).astype(BF16))
            ks.append(k * jnp.exp(tot - cum))
        if z == 0:
            before = [sum_tots(tots, 0, p) for p in range(gp)]
            after = [sum_tots(tots, p + 1, gp) for p in range(gp)]
        else:
            before = [sum_tots(tots, p + 1, gp) for p in range(gp)]
            after = [sum_tots(tots, 0, p) for p in range(gp)]
        for p in range(gp):
            qg_ref[z, sls[p], :] = (qd[p] * jnp.exp(before[p])).astype(BF16)
            kg_ref[z, sls[p], :] = (ks[p] * jnp.exp(after[p])).astype(BF16)
        dg_ref[z, pl.ds(g, 1), :] = jnp.exp(sum_tots(tots, 0, gp))
        scores = []
        for pq in range(gp):
            keys = []
            for pk in range(gp):
                seen = pk < pq if z == 0 else pk > pq
                if pk == pq:
                    keys.append(ki[pk])
                elif not seen:
                    keys.append(jnp.zeros_like(ki[pk]))
                else:
                    lo, hi = (pk + 1, pq) if z == 0 else (pq + 1, pk)
                    keys.append((ks[pk] * jnp.exp(sum_tots(tots, lo, hi))).astype(BF16))
            scores.append(lax.dot_general(qd[pq].astype(BF16), jnp.concatenate(keys, axis=0),
                                          nt, preferred_element_type=F32))
        return scores

    def local_out(z, g, scores):
        probs = []
        for pq, s in enumerate(scores):
            visible = (klane <= qrow + pq * c) if z == 0 else (klane > qrow + pq * c)
            probs.append(jnp.where(visible, s, 0.0).astype(BF16))
        rows = pl.ds(pl.multiple_of(g * gr, gr), gr)
        o_refs[z][rows, :] = jnp.dot(jnp.concatenate(probs, axis=0), v_ref[rows, :],
                                     preferred_element_type=F32)

    def local_body(g, carry):
        cums = [local_cums(z, g) for z in range(2)]
        scores = [local_scores(z, g, cums[z]) for z in range(2)]
        for z in range(2):
            local_out(z, g, scores[z])
        return carry

    lax.fori_loop(0, n_groups, local_body, 0)

    sf_ref[...] = jnp.zeros_like(sf_ref)
    sb_ref[...] = jnp.zeros_like(sb_ref)

    def state_body(it, carry):
        steps = []
        for z in range(2):
            for u in range(GLA_STATE_GROUP):
                i = it * GLA_STATE_GROUP + u
                g = i if z == 0 else n_groups - 1 - i
                rows = pl.ds(pl.multiple_of(g * gr, gr), gr)
                upd = lax.dot_general(v_ref[rows, :], kg_ref[z, rows, :], tn,
                                      preferred_element_type=F32)
                steps.append((z, g, rows, upd))
        st = [sf_ref[...], sb_ref[...]]
        for z, g, rows, upd in steps:
            o_refs[z][rows, :] += lax.dot_general(qg_ref[z, rows, :], st[z].astype(BF16), nt,
                                                  preferred_element_type=F32)
            st[z] = st[z] * dg_ref[z, pl.ds(g, 1), :] + upd
        sf_ref[...] = st[0]
        sb_ref[...] = st[1]
        return carry

    lax.fori_loop(0, n_groups // GLA_STATE_GROUP, state_body, 0)

    o = of_ref[...] + ob_ref[...]
    o = o * lax.rsqrt(jnp.mean(o * o, axis=-1, keepdims=True) + RMS_EPS) * ng_ref[...]
    gate = g_ref[...].astype(F32)
    o_ref[...] = (o * (gate * jax.nn.sigmoid(gate))).astype(o_ref.dtype)


def _gla_core(p, lr, w_gate, b_gate, norm_g, layer, batch):
    m = p.shape[0]
    t = m // batch
    h, dk, dv = GLA_HEADS, GLA_DK_HEAD, GLA_DV_HEAD
    return pl.pallas_call(
        _gla_kernel,
        grid=(batch, h),
        in_specs=[
            pl.BlockSpec((t, dk), lambda b, i: (b, i)),
            pl.BlockSpec((t, dk), lambda b, i: (b, h + i)),
            pl.BlockSpec((t, dv), lambda b, i: (b, h + i)),
            pl.BlockSpec((t, dv), lambda b, i: (b, 2 * h + i)),
            pl.BlockSpec((t, 2 * GLA_RANK), lambda b, i: (b, 0)),
            pl.BlockSpec((None, 2, GLA_RANK, dk), lambda b, i: (layer, 0, 0, i)),
            pl.BlockSpec((None, 2, 1, dk), lambda b, i: (layer, 0, 0, i)),
            pl.BlockSpec((None, 1, dv), lambda b, i: (layer, 0, 0)),
        ],
        out_specs=pl.BlockSpec((t, dv), lambda b, i: (b, i)),
        out_shape=jax.ShapeDtypeStruct((m, GLA_DV), BF16),
        scratch_shapes=[pltpu.VMEM((2, t, dk), F32),
                        pltpu.VMEM((2, t, dk), BF16), pltpu.VMEM((2, t, dk), BF16),
                        pltpu.VMEM((2, t // (GLA_CHUNK * GLA_GROUP), dk), F32),
                        pltpu.VMEM((t, dv), F32), pltpu.VMEM((t, dv), F32),
                        pltpu.VMEM((dv, dk), F32), pltpu.VMEM((dv, dk), F32)],
        compiler_params=_params("parallel", "arbitrary"),
        name="gla_core",
    )(p, p, p, p, lr, w_gate, b_gate, norm_g)


NA_DR = 2 * NA_KR
NA_DC = 2 * NA_KC - 1
NA_PAIR_W = 2 * GRID_W


def _na_col_geometry(shape):
    log_w = GRID_W.bit_length() - 1
    assert 1 << log_w == GRID_W
    col = lax.broadcasted_iota(jnp.int32, shape, 1)
    qc = col >> (log_w + 1)
    half = (col >> log_w) & 1
    kc = col & (GRID_W - 1)
    win = jnp.clip(qc - NA_KC // 2, 0, GRID_W - NA_KC)
    valid = (kc >= win) & (kc < win + NA_KC)
    src = half * NA_DC + (kc - qc + NA_KC - 1)
    return src, valid


def _na_bias_kernel(rpb2_ref, o_ref, sel_ref, valid_ref):
    n = o_ref.shape[1]

    @pl.when((pl.program_id(0) == 0) & (pl.program_id(1) == 0))
    def _build_selector():
        src, valid = _na_col_geometry((GRID_W, n))
        r = lax.broadcasted_iota(jnp.int32, (GRID_W, n), 0)
        sel_ref[...] = jnp.where(valid & (r == src), 1.0, 0.0).astype(BF16)
        _, valid_o = _na_col_geometry((NA_DR, n))
        valid_ref[...] = jnp.where(valid_o, 1.0, 0.0)

    sel = sel_ref[...]
    acc = jnp.zeros(o_ref.shape, F32)
    for term in _bf16_terms(rpb2_ref[...], 3):
        acc = acc + jnp.dot(term, sel, preferred_element_type=F32)
    o_ref[...] = jnp.where(valid_ref[...] > 0.5, acc, NEG_INF)


def _na_bias_tiles(rpb):
    nl, nh = rpb.shape[:2]
    assert rpb.shape[2] == NA_DR - 1 and rpb.shape[3] == NA_DC and 2 * NA_DC <= GRID_W
    rp = jnp.pad(rpb.astype(F32), ((0, 0), (0, 0), (0, 2), (0, 0)))
    pad = jnp.zeros((nl, nh, NA_DR, GRID_W - 2 * NA_DC), F32)
    rpb2 = jnp.concatenate([rp[:, :, :NA_DR], rp[:, :, 1:NA_DR + 1], pad], axis=-1)
    n = GRID_W * NA_PAIR_W
    tiles = pl.pallas_call(
        _na_bias_kernel,
        grid=(nl, nh),
        in_specs=[pl.BlockSpec((None, None, NA_DR, GRID_W), lambda l, h: (l, h, 0, 0))],
        out_specs=pl.BlockSpec((None, None, NA_DR, n), lambda l, h: (l, h, 0, 0)),
        out_shape=jax.ShapeDtypeStruct((nl, nh, NA_DR, n), F32),
        scratch_shapes=[pltpu.VMEM((GRID_W, n), BF16), pltpu.VMEM((NA_DR, n), F32)],
        compiler_params=_params("arbitrary", "arbitrary"),
        name="na_bias",
    )(rpb2)
    return tiles.reshape(nl, nh, NA_DR, GRID_W, NA_PAIR_W)


def _na_kernel(q_ref, k_ref, v_ref, bias_ref, o_ref):
    rows = q_ref.shape[0] // GRID_W
    band = NA_KR * GRID_W
    scale = NA_HD ** -0.5

    def body(it, carry):
        group = []
        for u in range(NA_ROW_GROUP):
            r = it * NA_ROW_GROUP + u
            rs = jnp.clip(r - NA_KR // 2, 0, rows - NA_KR)
            d0 = rs - r + NA_KR - 1
            qs = pl.ds(pl.multiple_of(r * GRID_W, GRID_W), GRID_W)
            ks = pl.ds(pl.multiple_of(rs * GRID_W, GRID_W), band)
            s = lax.dot_general(q_ref[qs, :], k_ref[ks, :], (((1,), (1,)), ((), ())),
                                preferred_element_type=F32)
            group.append((qs, ks, d0, s))
        probs = []
        for qs, ks, d0, s in group:
            bias = jnp.concatenate([bias_ref[d0 + 2 * p] for p in range(NA_KR // 2)], axis=1)
            s = s * scale + bias
            p = jnp.exp(s - jnp.max(s, axis=-1, keepdims=True))
            l = jnp.sum(p, axis=-1, keepdims=True)
            probs.append((qs, ks, p.astype(BF16), l))
        for qs, ks, p, l in probs:
            o = jnp.dot(p, v_ref[ks, :], preferred_element_type=F32)
            o_ref[qs, :] = (o / l).astype(o_ref.dtype)
        return carry

    lax.fori_loop(0, rows // NA_ROW_GROUP, body, 0)


def _na_core(qkv, bias, layer, batch):
    m = qkv.shape[0]
    t = m // batch
    h, hd = NA_HEADS, NA_HD
    return pl.pallas_call(
        _na_kernel,
        grid=(batch, h),
        in_specs=[
            pl.BlockSpec((t, hd), lambda b, i: (b, i)),
            pl.BlockSpec((t, hd), lambda b, i: (b, h + i)),
            pl.BlockSpec((t, hd), lambda b, i: (b, 2 * h + i)),
            pl.BlockSpec((None, None, NA_DR, GRID_W, NA_PAIR_W),
                         lambda b, i: (layer, i, 0, 0, 0)),
        ],
        out_specs=pl.BlockSpec((t, hd), lambda b, i: (b, i)),
        out_shape=jax.ShapeDtypeStruct((m, D_MODEL), BF16),
        compiler_params=_params("parallel", "arbitrary"),
        name="na_core",
    )(qkv, qkv, qkv, bias)


def kernel(x, gla_w_in, gla_w_gate_up, gla_b_gate, gla_norm_g, gla_w_out, na_w_in, na_rpb, na_w_out, ffn_w_up, ffn_conv_w, ffn_conv_b, ffn_w_down, ln_mix_g, ln_mix_b, ln_ffn_g, ln_ffn_b):
    batch, seq, d = x.shape
    assert seq == SEQ and d == D_MODEL
    n_main = 2 * GLA_DK + 2 * GLA_DV
    gla_w = gla_w_in.astype(BF16)
    gla_w_lr = gla_w_in[:, :, n_main:].astype(BF16)
    gla_wo = gla_w_out.astype(BF16)
    gla_bg = gla_b_gate.reshape(gla_b_gate.shape[0], 2, 1, GLA_DK)
    gla_ng = gla_norm_g.reshape(gla_norm_g.shape[0], 1, GLA_DV_HEAD)
    na_w = na_w_in.astype(BF16)
    na_wo = na_w_out.astype(BF16)
    na_bias = _na_bias_tiles(na_rpb)
    w_up = _tile_major_up(ffn_w_up, FFN_TF)
    w_down = ffn_w_down.astype(BF16)
    conv_b = ffn_conv_b.reshape(DEPTH, 1, 2 * D_FF)
    mix_g, mix_b = ln_mix_g.reshape(DEPTH, 1, d), ln_mix_b.reshape(DEPTH, 1, d)
    ffn_g, ffn_b = ln_ffn_g.reshape(DEPTH, 1, d), ln_ffn_b.reshape(DEPTH, 1, d)

    xf = x.reshape(batch * seq, d)
    for i in range(DEPTH):
        j = i // 2
        if i % 2 == 0:
            p, lr = _matmul(xf, gla_w, j, n_main, BF16, tm=PROJ_TM, tn=PROJ_TN, w_side=gla_w_lr)
            mix = _gla_core(p, lr, gla_w_gate_up, gla_bg, gla_ng, j, batch)
            w_out = gla_wo
        else:
            qkv = _matmul(xf, na_w, j, 3 * d, BF16, tm=PROJ_TM, tn=PROJ_TN)
            mix = _na_core(qkv, na_bias, j, batch)
            w_out = na_wo
        xf = _matmul_res_ln(mix, w_out, j, xf, mix_g, mix_b, i, tm=512)
        xf = _conv_ffn_ln(xf, w_up, ffn_conv_w, conv_b, w_down, ffn_g, ffn_b, i,
                          tm=FFN_TM, tf=FFN_TF)
    return xf.reshape(batch, seq, d)
```

```python
import functools
import math

import jax
import jax.numpy as jnp
import numpy as np
from jax import lax
from jax.experimental import pallas as pl
from jax.experimental.pallas import tpu as pltpu

F32 = jnp.float32
BF16 = jnp.bfloat16

D_MODEL = 2048
SEQ = 2048
DEPTH = 4
GRID_W = 64
GLA_HEADS = 4
GLA_DK = D_MODEL // 2
GLA_DV = D_MODEL
GLA_DK_HEAD = GLA_DK // GLA_HEADS
GLA_DV_HEAD = GLA_DV // GLA_HEADS
GLA_RANK = 16
GLA_TEMP = 16.0
GLA_CHUNK = 64
GLA_GROUP = 4
GLA_STATE_GROUP = 2
NA_HEADS = 16
NA_HD = D_MODEL // NA_HEADS
NA_KR = 8
NA_KC = 16
NA_ROW_GROUP = 8
D_FF = 11 * D_MODEL // 4
LN_EPS = 1e-5
RMS_EPS = 1e-6
NEG_INF = -1e9
ALPHA = (2.0 * DEPTH) ** 0.25
LOG2E = math.log2(math.e)

V7X_VMEM_BYTES = 64 * 1024 * 1024
VMEM_LIMIT = V7X_VMEM_BYTES - 8 * 1024 * 1024
LN_ROW_SPLIT = 4
PROJ_TM = 1024
PROJ_TN = 2048
FFN_TM = 1024
FFN_TF = 512
FFN_SUB = 256
HALO = 16


def _params(*sem):
    return pltpu.CompilerParams(dimension_semantics=sem, vmem_limit_bytes=VMEM_LIMIT)


def _layer_norm(y, g, b):
    mu = jnp.mean(y, axis=-1, keepdims=True)
    yc = y - mu
    var = jnp.mean(yc * yc, axis=-1, keepdims=True)
    return yc * lax.rsqrt(var + LN_EPS) * g + b


def _mm_kernel(x_ref, w_ref, o_ref):
    x = x_ref[...].astype(BF16)
    o_ref[...] = jnp.dot(x, w_ref[...], preferred_element_type=F32).astype(o_ref.dtype)


def _mm_side_kernel(x_ref, w_ref, ws_ref, o_ref, os_ref):
    x = x_ref[...].astype(BF16)
    o_ref[...] = jnp.dot(x, w_ref[...], preferred_element_type=F32).astype(o_ref.dtype)

    @pl.when(pl.program_id(1) == 0)
    def _side():
        os_ref[...] = jnp.dot(x, ws_ref[...], preferred_element_type=F32)


def _matmul(x, w, layer, n, out_dtype, tm, tn, w_side=None):
    m, k = x.shape
    assert m % tm == 0 and n % tn == 0 and n <= w.shape[2]
    in_specs = [pl.BlockSpec((tm, k), lambda i, j: (i, 0)),
                pl.BlockSpec((None, k, tn), lambda i, j: (layer, 0, j))]
    out_specs = pl.BlockSpec((tm, tn), lambda i, j: (i, j))
    out_shape = jax.ShapeDtypeStruct((m, n), out_dtype)
    if w_side is None:
        body, operands = _mm_kernel, (x, w)
    else:
        s = w_side.shape[2]
        body, operands = _mm_side_kernel, (x, w, w_side)
        in_specs.append(pl.BlockSpec((None, k, s), lambda i, j: (layer, 0, 0)))
        out_specs = (out_specs, pl.BlockSpec((tm, s), lambda i, j: (i, 0)))
        out_shape = (out_shape, jax.ShapeDtypeStruct((m, s), F32))
    return pl.pallas_call(
        body,
        grid=(m // tm, n // tn),
        in_specs=in_specs,
        out_specs=out_specs,
        out_shape=out_shape,
        compiler_params=_params("parallel", "arbitrary"),
        name="in_proj",
    )(*operands)


def _mm_ln_kernel(a_ref, w_ref, x_ref, g_ref, b_ref, o_ref):
    sub = a_ref.shape[0] // LN_ROW_SPLIT
    for r in range(LN_ROW_SPLIT):
        rs = slice(r * sub, (r + 1) * sub)
        m = jnp.dot(a_ref[rs, :], w_ref[...], preferred_element_type=F32)
        y = ALPHA * x_ref[rs, :] + m
        o_ref[rs, :] = _layer_norm(y, g_ref[...], b_ref[...])


def _matmul_res_ln(a, w, w_layer, x, g, b, ln_layer, tm):
    m, k = a.shape
    d = w.shape[2]
    return pl.pallas_call(
        _mm_ln_kernel,
        grid=(m // tm,),
        in_specs=[pl.BlockSpec((tm, k), lambda i: (i, 0)),
                  pl.BlockSpec((None, k, d), lambda i: (w_layer, 0, 0)),
                  pl.BlockSpec((tm, d), lambda i: (i, 0)),
                  pl.BlockSpec((None, 1, d), lambda i: (ln_layer, 0, 0)),
                  pl.BlockSpec((None, 1, d), lambda i: (ln_layer, 0, 0))],
        out_specs=pl.BlockSpec((tm, d), lambda i: (i, 0)),
        out_shape=jax.ShapeDtypeStruct((m, d), F32),
        compiler_params=_params("parallel"),
        name="out_proj_ln",
    )(a, w, x, g, b)


def _gelu(a):
    return 0.5 * a * (1.0 + lax.erf(a * (2.0 ** -0.5)))


def _ffn_kernel(xp_ref, x_hbm, xn_ref, wa_ref, wb_ref, cw_ref, cb_ref, wd_ref, g_ref, b_ref,
                o_ref, xe_ref, xbuf_ref, x_sem, *, tm, tiles_per_seq):
    i = pl.program_id(0)
    j = pl.program_id(1)
    ni = pl.num_programs(0)
    nj = pl.num_programs(1)
    rows = tm + 2 * HALO
    tf = wd_ref.shape[0]
    d_ff = cw_ref.shape[1] // 2

    def x_copy(tile):
        return pltpu.make_async_copy(x_hbm.at[pl.ds(tile * tm, tm), :], xbuf_ref, x_sem)

    @pl.when(j == 0)
    def _stage_rows():
        @pl.when(i == 0)
        def _first_tile():
            x_copy(0).start()

        x_copy(i).wait()
        pos = i % tiles_per_seq
        keep_prev = (pos != 0).astype(F32)
        keep_next = (pos != tiles_per_seq - 1).astype(F32)
        zeros = jnp.zeros((HALO - 8, xbuf_ref.shape[1]), F32)
        prev = jnp.concatenate([zeros, xp_ref[...] * keep_prev], axis=0)
        nxt = jnp.concatenate([xn_ref[...] * keep_next, zeros], axis=0)
        xe_ref[0:HALO, :] = prev.astype(BF16)
        xe_ref[HALO:HALO + tm, :] = xbuf_ref[...].astype(BF16)
        xe_ref[HALO + tm:rows, :] = nxt.astype(BF16)
        o_ref[...] = ALPHA * xbuf_ref[...]

    @pl.when((j == 1) & (i + 1 < ni))
    def _prefetch_next_tile():
        x_copy(i + 1).start()

    xe = xe_ref[...]

    def conv_half(w_ref, col, p_col):
        h = jnp.dot(xe, w_ref[:, col:col + FFN_SUB], preferred_element_type=F32)
        h_prev = pltpu.roll(h, 1, 0)[HALO:HALO + tm]
        h_next = pltpu.roll(h, rows - 1, 0)[HALO:HALO + tm]
        ps = pl.ds(pl.multiple_of(p_col, FFN_SUB), FFN_SUB)
        cw = cw_ref[:, ps]
        return (h_prev * cw[0:1] + h[HALO:HALO + tm] * cw[1:2] + h_next * cw[2:3]
                + cb_ref[:, ps])

    contrib = None
    for c in range(tf // FFN_SUB):
        col = c * FFN_SUB
        a = conv_half(wa_ref, col, j * tf + col)
        bb = conv_half(wb_ref, col, d_ff + j * tf + col)
        u = (_gelu(a) * bb).astype(BF16)
        part = jnp.dot(u, wd_ref[col:col + FFN_SUB, :], preferred_element_type=F32)
        contrib = part if contrib is None else contrib + part
    o_ref[...] += contrib

    @pl.when(j == nj - 1)
    def _finish():
        o_ref[...] = _layer_norm(o_ref[...], g_ref[...], b_ref[...])


def _conv_ffn_ln(x, w_up, conv_w, conv_b, w_down, g, b, layer, tm, tf):
    m, d = x.shape
    f = w_down.shape[1]
    nf = f // tf
    assert m % tm == 0 and f % tf == 0 and SEQ % tm == 0 and tm % 8 == 0
    assert tf % FFN_SUB == 0 and nf >= 2
    tb = tm // 8
    last_blk = m // 8 - 1
    kern = functools.partial(_ffn_kernel, tm=tm, tiles_per_seq=SEQ // tm)
    return pl.pallas_call(
        kern,
        grid=(m // tm, nf),
        in_specs=[
            pl.BlockSpec((8, d), lambda i, j: (jnp.maximum(i * tb - 1, 0), 0)),
            pl.BlockSpec(memory_space=pl.ANY),
            pl.BlockSpec((8, d), lambda i, j: (jnp.minimum((i + 1) * tb, last_blk), 0)),
            pl.BlockSpec((None, d, tf), lambda i, j: (layer, 0, j)),
            pl.BlockSpec((None, d, tf), lambda i, j: (layer, 0, j + nf)),
            pl.BlockSpec((None, 3, 2 * f), lambda i, j: (layer, 0, 0)),
            pl.BlockSpec((None, 1, 2 * f), lambda i, j: (layer, 0, 0)),
            pl.BlockSpec((None, tf, d), lambda i, j: (layer, j, 0)),
            pl.BlockSpec((None, 1, d), lambda i, j: (layer, 0, 0)),
            pl.BlockSpec((None, 1, d), lambda i, j: (layer, 0, 0)),
        ],
        out_specs=pl.BlockSpec((tm, d), lambda i, j: (i, 0), pipeline_mode=pl.Buffered(1)),
        out_shape=jax.ShapeDtypeStruct((m, d), F32),
        scratch_shapes=[pltpu.VMEM((tm + 2 * HALO, d), BF16),
                        pltpu.VMEM((tm, d), F32),
                        pltpu.SemaphoreType.DMA(())],
        compiler_params=_params("arbitrary", "arbitrary"),
        name="conv_ffn_ln",
    )(x, x, x, w_up, w_up, conv_w, conv_b, w_down, g, b)


def _bf16_terms(x, n_terms):
    terms = []
    for _ in range(n_terms - 1):
        t = x.astype(BF16)
        terms.append(t)
        x = x - t.astype(F32)
    terms.append(x.astype(BF16))
    return terms


def _split_dot(tri, x):
    t = tri.astype(BF16)
    hi, lo = _bf16_terms(x, 2)
    return (jnp.dot(t, hi, preferred_element_type=F32)
            + jnp.dot(t, lo, preferred_element_type=F32))


def _log_sigmoid(x):
    return jnp.minimum(x, 0.0) - jnp.log(1.0 + jnp.exp(-jnp.abs(x)))


def _gla_kernel(q_ref, k_ref, v_ref, g_ref, lr_ref, wg_ref, bg_ref, ng_ref, o_ref,
                la_ref, qg_ref, kg_ref, dg_ref, of_ref, ob_ref, sf_ref, sb_ref):
    t = q_ref.shape[0]
    c = GLA_CHUNK
    gp = GLA_GROUP
    gr = gp * c
    n_groups = t // gr
    scale = GLA_DK_HEAD ** -0.5
    nt = (((1,), (1,)), ((), ()))
    tn = (((0,), (0,)), ((), ()))

    lr = lr_ref[...]
    for z in range(2):
        logit = jnp.dot(lr[:, z * GLA_RANK:(z + 1) * GLA_RANK].astype(BF16),
                        wg_ref[z].astype(BF16), preferred_element_type=F32)
        la_ref[z] = _log_sigmoid(logit + bg_ref[z]) * (1.0 / GLA_TEMP)

    row = lax.broadcasted_iota(jnp.int32, (c, c), 0)
    col = lax.broadcasted_iota(jnp.int32, (c, c), 1)
    tri = ((row >= col).astype(F32), (row <= col).astype(F32))
    tot_row = (c - 1, 0)
    qrow = lax.broadcasted_iota(jnp.int32, (c, gr), 0)
    klane = lax.broadcasted_iota(jnp.int32, (c, gr), 1)
    s_refs = (sf_ref, sb_ref)
    o_refs = (of_ref, ob_ref)

    def sum_tots(tots, lo, hi):
        acc = jnp.zeros_like(tots[0])
        for p in range(lo, hi):
            acc = acc + tots[p]
        return acc

    def chunk_slices(g):
        return [pl.ds(pl.multiple_of(g * gr + p * c, c), c) for p in range(gp)]

    def local_cums(z, g):
        return [_split_dot(tri[z], la_ref[z, sl, :]) for sl in chunk_slices(g)]

    def local_scores(z, g, cums):
        sls = chunk_slices(g)
        tots = [cum[tot_row[z]:tot_row[z] + 1, :] for cum in cums]
        qd, ki, ks = [], [], []
        for sl, cum, tot in zip(sls, cums, tots):
            q = q_ref[sl, :].astype(F32) * scale
            k = k_ref[sl, :].astype(F32)
            qd.append(q * jnp.exp(cum))
            ki.append((k * jnp.exp(-cum)).astype(BF16))
            ks.append(k * jnp.exp(tot - cum))
        if z == 0:
            before = [sum_tots(tots, 0, p) for p in range(gp)]
            after = [sum_tots(tots, p + 1, gp) for p in range(gp)]
        else:
            before = [sum_tots(tots, p + 1, gp) for p in range(gp)]
            after = [sum_tots(tots, 0, p) for p in range(gp)]
        for p in range(gp):
            qg_ref[z, sls[p], :] = (qd[p] * jnp.exp(before[p])).astype(BF16)
            kg_ref[z, sls[p], :] = (ks[p] * jnp.exp(after[p])).astype(BF16)
        dg_ref[z, pl.ds(g, 1), :] = jnp.exp(sum_tots(tots, 0, gp))
        scores = []
        for pq in range(gp):
            keys = []
            for pk in range(gp):
                seen = pk < pq if z == 0 else pk > pq
                if pk == pq:
                    keys.append(ki[pk])
                elif not seen:
                    keys.append(jnp.zeros_like(ki[pk]))
                else:
                    lo, hi = (pk + 1, pq) if z == 0 else (pq + 1, pk)
                    keys.append((ks[pk] * jnp.exp(sum_tots(tots, lo, hi))).astype(BF16))
            scores.append(lax.dot_general(qd[pq].astype(BF16), jnp.concatenate(keys, axis=0),
                                          nt, preferred_element_type=F32))
        return scores

    def local_out(z, g, scores):
        probs = []
        for pq, s in enumerate(scores):
            visible = (klane <= qrow + pq * c) if z == 0 else (klane > qrow + pq * c)
            probs.append(jnp.where(visible, s, 0.0).astype(BF16))
        rows = pl.ds(pl.multiple_of(g * gr, gr), gr)
        o_refs[z][rows, :] = jnp.dot(jnp.concatenate(probs, axis=0), v_ref[rows, :],
                                     preferred_element_type=F32)

    def local_body(g, carry):
        cums = [local_cums(z, g) for z in range(2)]
        scores = [local_scores(z, g, cums[z]) for z in range(2)]
        for z in range(2):
            local_out(z, g, scores[z])
        return carry

    lax.fori_loop(0, n_groups, local_body, 0)

    sf_ref[...] = jnp.zeros_like(sf_ref)
    sb_ref[...] = jnp.zeros_like(sb_ref)

    def state_body(it, carry):
        steps = []
        for z in range(2):
            for u in range(GLA_STATE_GROUP):
                i = it * GLA_STATE_GROUP + u
                g = i if z == 0 else n_groups - 1 - i
                rows = pl.ds(pl.multiple_of(g * gr, gr), gr)
                upd = lax.dot_general(v_ref[rows, :], kg_ref[z, rows, :], tn,
                                      preferred_element_type=F32)
                steps.append((z, g, rows, upd))
        st = [sf_ref[...], sb_ref[...]]
        for z, g, rows, upd in steps:
            o_refs[z][rows, :] += lax.dot_general(qg_ref[z, rows, :], st[z].astype(BF16), nt,
                                                  preferred_element_type=F32)
            st[z] = st[z] * dg_ref[z, pl.ds(g, 1), :] + upd
        sf_ref[...] = st[0]
        sb_ref[...] = st[1]
        return carry

    lax.fori_loop(0, n_groups // GLA_STATE_GROUP, state_body, 0)

    o = of_ref[...] + ob_ref[...]
    o = o * lax.rsqrt(jnp.mean(o * o, axis=-1, keepdims=True) + RMS_EPS) * ng_ref[...]
    gate = g_ref[...].astype(F32)
    o_ref[...] = (o * (gate * jax.nn.sigmoid(gate))).astype(o_ref.dtype)


def _gla_core(p, lr, w_gate, b_gate, norm_g, layer, batch):
    m = p.shape[0]
    t = m // batch
    h, dk, dv = GLA_HEADS, GLA_DK_HEAD, GLA_DV_HEAD
    return pl.pallas_call(
        _gla_kernel,
        grid=(batch, h),
        in_specs=[
            pl.BlockSpec((t, dk), lambda b, i: (b, i)),
            pl.BlockSpec((t, dk), lambda b, i: (b, h + i)),
            pl.BlockSpec((t, dv), lambda b, i: (b, h + i)),
            pl.BlockSpec((t, dv), lambda b, i: (b, 2 * h + i)),
            pl.BlockSpec((t, 2 * GLA_RANK), lambda b, i: (b, 0)),
            pl.BlockSpec((None, 2, GLA_RANK, dk), lambda b, i: (layer, 0, 0, i)),
            pl.BlockSpec((None, 2, 1, dk), lambda b, i: (layer, 0, 0, i)),
            pl.BlockSpec((None, 1, dv), lambda b, i: (layer, 0, 0)),
        ],
        out_specs=pl.BlockSpec((t, dv), lambda b, i: (b, i)),
        out_shape=jax.ShapeDtypeStruct((m, GLA_DV), BF16),
        scratch_shapes=[pltpu.VMEM((2, t, dk), F32),
                        pltpu.VMEM((2, t, dk), BF16), pltpu.VMEM((2, t, dk), BF16),
                        pltpu.VMEM((2, t // (GLA_CHUNK * GLA_GROUP), dk), F32),
                        pltpu.VMEM((t, dv), F32), pltpu.VMEM((t, dv), F32),
                        pltpu.VMEM((dv, dk), F32), pltpu.VMEM((dv, dk), F32)],
        compiler_params=_params("parallel", "arbitrary"),
        name="gla_core",
    )(p, p, p, p, lr, w_gate, b_gate, norm_g)


NA_DR = 2 * NA_KR
NA_DC = 2 * NA_KC - 1
NA_PAIR_W = 2 * GRID_W


def _na_col_geometry(shape):
    log_w = GRID_W.bit_length() - 1
    assert 1 << log_w == GRID_W
    col = lax.broadcasted_iota(jnp.int32, shape, 1)
    qc = col >> (log_w + 1)
    half = (col >> log_w) & 1
    kc = col & (GRID_W - 1)
    win = jnp.clip(qc - NA_KC // 2, 0, GRID_W - NA_KC)
    valid = (kc >= win) & (kc < win + NA_KC)
    src = half * NA_DC + (kc - qc + NA_KC - 1)
    return src, valid


def _na_bias_kernel(rpb2_ref, o_ref, sel_ref, valid_ref):
    n = o_ref.shape[1]

    @pl.when((pl.program_id(0) == 0) & (pl.program_id(1) == 0))
    def _build_selector():
        src, valid = _na_col_geometry((GRID_W, n))
        r = lax.broadcasted_iota(jnp.int32, (GRID_W, n), 0)
        sel_ref[...] = jnp.where(valid & (r == src), 1.0, 0.0).astype(BF16)
        _, valid_o = _na_col_geometry((NA_DR, n))
        valid_ref[...] = jnp.where(valid_o, 1.0, 0.0)

    sel = sel_ref[...]
    acc = jnp.zeros(o_ref.shape, F32)
    for term in _bf16_terms(rpb2_ref[...], 3):
        acc = acc + jnp.dot(term, sel, preferred_element_type=F32)
    o_ref[...] = jnp.where(valid_ref[...] > 0.5, acc, NEG_INF) * LOG2E


def _na_bias_tiles(rpb):
    nl, nh = rpb.shape[:2]
    assert rpb.shape[2] == NA_DR - 1 and rpb.shape[3] == NA_DC and 2 * NA_DC <= GRID_W
    rp = jnp.pad(rpb.astype(F32), ((0, 0), (0, 0), (0, 2), (0, 0)))
    pad = jnp.zeros((nl, nh, NA_DR, GRID_W - 2 * NA_DC), F32)
    rpb2 = jnp.concatenate([rp[:, :, :NA_DR], rp[:, :, 1:NA_DR + 1], pad], axis=-1)
    n = GRID_W * NA_PAIR_W
    tiles = pl.pallas_call(
        _na_bias_kernel,
        grid=(nl, nh),
        in_specs=[pl.BlockSpec((None, None, NA_DR, GRID_W), lambda l, h: (l, h, 0, 0))],
        out_specs=pl.BlockSpec((None, None, NA_DR, n), lambda l, h: (l, h, 0, 0)),
        out_shape=jax.ShapeDtypeStruct((nl, nh, NA_DR, n), F32),
        scratch_shapes=[pltpu.VMEM((GRID_W, n), BF16), pltpu.VMEM((NA_DR, n), F32)],
        compiler_params=_params("arbitrary", "arbitrary"),
        name="na_bias",
    )(rpb2)
    return tiles.reshape(nl, nh, NA_DR, GRID_W, NA_PAIR_W)


def _na_kernel(q_ref, k_ref, v_ref, bias_ref, o_ref):
    rows = q_ref.shape[0] // GRID_W
    band = NA_KR * GRID_W
    scale = NA_HD ** -0.5 * LOG2E

    def body(it, carry):
        group = []
        for u in range(NA_ROW_GROUP):
            r = it * NA_ROW_GROUP + u
            rs = jnp.clip(r - NA_KR // 2, 0, rows - NA_KR)
            d0 = rs - r + NA_KR - 1
            qs = pl.ds(pl.multiple_of(r * GRID_W, GRID_W), GRID_W)
            ks = pl.ds(pl.multiple_of(rs * GRID_W, GRID_W), band)
            s = lax.dot_general(q_ref[qs, :], k_ref[ks, :], (((1,), (1,)), ((), ())),
                                preferred_element_type=F32)
            group.append((qs, ks, d0, s))
        probs = []
        for qs, ks, d0, s in group:
            bias = jnp.concatenate([bias_ref[d0 + 2 * p] for p in range(NA_KR // 2)], axis=1)
            s = s * scale + bias
            p = jnp.exp2(s - jnp.max(s, axis=-1, keepdims=True))
            l = jnp.sum(p, axis=-1, keepdims=True)
            probs.append((qs, ks, p.astype(BF16), l))
        for qs, ks, p, l in probs:
            o = jnp.dot(p, v_ref[ks, :], preferred_element_type=F32)
            o_ref[qs, :] = (o / l).astype(o_ref.dtype)
        return carry

    lax.fori_loop(0, rows // NA_ROW_GROUP, body, 0)


def _na_core(qkv, bias, layer, batch):
    m = qkv.shape[0]
    t = m // batch
    h, hd = NA_HEADS, NA_HD
    return pl.pallas_call(
        _na_kernel,
        grid=(batch, h),
        in_specs=[
            pl.BlockSpec((t, hd), lambda b, i: (b, i)),
            pl.BlockSpec((t, hd), lambda b, i: (b, h + i)),
            pl.BlockSpec((t, hd), lambda b, i: (b, 2 * h + i)),
            pl.BlockSpec((None, None, NA_DR, GRID_W, NA_PAIR_W),
                         lambda b, i: (layer, i, 0, 0, 0)),
        ],
        out_specs=pl.BlockSpec((t, hd), lambda b, i: (b, i)),
        out_shape=jax.ShapeDtypeStruct((m, D_MODEL), BF16),
        compiler_params=_params("parallel", "arbitrary"),
        name="na_core",
    )(qkv, qkv, qkv, bias)


def kernel(x, gla_w_in, gla_w_gate_up, gla_b_gate, gla_norm_g, gla_w_out, na_w_in, na_rpb, na_w_out, ffn_w_up, ffn_conv_w, ffn_conv_b, ffn_w_down, ln_mix_g, ln_mix_b, ln_ffn_g, ln_ffn_b):
    batch, seq, d = x.shape
    assert seq == SEQ and d == D_MODEL
    n_main = 2 * GLA_DK + 2 * GLA_DV
    gla_w = gla_w_in.astype(BF16)
    gla_w_lr = gla_w_in[:, :, n_main:].astype(BF16)
    gla_wo = gla_w_out.astype(BF16)
    gla_bg = gla_b_gate.reshape(gla_b_gate.shape[0], 2, 1, GLA_DK)
    gla_ng = gla_norm_g.reshape(gla_norm_g.shape[0], 1, GLA_DV_HEAD)
    na_w = na_w_in.astype(BF16)
    na_wo = na_w_out.astype(BF16)
    na_bias = _na_bias_tiles(na_rpb)
    w_up = ffn_w_up.astype(BF16)
    w_down = ffn_w_down.astype(BF16)
    conv_b = ffn_conv_b.reshape(DEPTH, 1, 2 * D_FF)
    mix_g, mix_b = ln_mix_g.reshape(DEPTH, 1, d), ln_mix_b.reshape(DEPTH, 1, d)
    ffn_g, ffn_b = ln_ffn_g.reshape(DEPTH, 1, d), ln_ffn_b.reshape(DEPTH, 1, d)

    xf = x.reshape(batch * seq, d)
    for i in range(DEPTH):
        j = i // 2
        if i % 2 == 0:
            p, lr = _matmul(xf, gla_w, j, n_main, BF16, tm=PROJ_TM, tn=PROJ_TN, w_side=gla_w_lr)
            mix = _gla_core(p, lr, gla_w_gate_up, gla_bg, gla_ng, j, batch)
            w_out = gla_wo
        else:
            qkv = _matmul(xf, na_w, j, 3 * d, BF16, tm=PROJ_TM, tn=PROJ_TN)
            mix = _na_core(qkv, na_bias, j, batch)
            w_out = na_wo
        xf = _matmul_res_ln(mix, w_out, j, xf, mix_g, mix_b, i, tm=512)
        xf = _conv_ffn_ln(xf, w_up, ffn_conv_w, conv_b, w_down, ffn_g, ffn_b, i,
                          tm=FFN_TM, tf=FFN_TF)
    return xf.reshape(batch, seq, d)
```

```python
import functools
import math

import jax
import jax.numpy as jnp
import numpy as np
from jax import lax
from jax.experimental import pallas as pl
from jax.experimental.pallas import tpu as pltpu

F32 = jnp.float32
BF16 = jnp.bfloat16

D_MODEL = 2048
SEQ = 2048
DEPTH = 4
GRID_W = 64
GLA_HEADS = 4
GLA_DK = D_MODEL // 2
GLA_DV = D_MODEL
GLA_DK_HEAD = GLA_DK // GLA_HEADS
GLA_DV_HEAD = GLA_DV // GLA_HEADS
GLA_RANK = 16
GLA_TEMP = 16.0
GLA_CHUNK = 64
GLA_GROUP = 4
GLA_LOCAL_GROUPS = 2
GLA_STATE_GROUP = 2
NA_HEADS = 16
NA_HD = D_MODEL // NA_HEADS
NA_KR = 8
NA_KC = 16
NA_HEADS_PER_STEP = 4
NA_ROW_GROUP = 8
D_FF = 11 * D_MODEL // 4
LN_EPS = 1e-5
RMS_EPS = 1e-6
NEG_INF = -1e9
ALPHA = (2.0 * DEPTH) ** 0.25
LOG2E = math.log2(math.e)

V7X_VMEM_BYTES = 64 * 1024 * 1024
VMEM_LIMIT = V7X_VMEM_BYTES - 8 * 1024 * 1024
LN_ROW_SPLIT = 4
OUT_TM = 512
PROJ_TM = 1024
PROJ_TN = 2048
FFN_TM = 1024
FFN_TF = 512
FFN_SUB = 256
HALO = 16


def _params(*sem):
    return pltpu.CompilerParams(dimension_semantics=sem, vmem_limit_bytes=VMEM_LIMIT)


def _layer_norm(y, g, b):
    mu = jnp.mean(y, axis=-1, keepdims=True)
    yc = y - mu
    var = jnp.mean(yc * yc, axis=-1, keepdims=True)
    return yc * lax.rsqrt(var + LN_EPS) * g + b


def _mm_kernel(x_ref, w_ref, o_ref):
    x = x_ref[...].astype(BF16)
    o_ref[...] = jnp.dot(x, w_ref[...], preferred_element_type=F32).astype(o_ref.dtype)


def _mm_side_kernel(x_ref, w_ref, ws_ref, o_ref, os_ref):
    x = x_ref[...].astype(BF16)
    o_ref[...] = jnp.dot(x, w_ref[...], preferred_element_type=F32).astype(o_ref.dtype)

    @pl.when(pl.program_id(1) == 0)
    def _side():
        os_ref[...] = jnp.dot(x, ws_ref[...], preferred_element_type=F32)


def _matmul(x, w, layer, n, out_dtype, tm, tn, w_side=None):
    m, k = x.shape
    assert m % tm == 0 and n % tn == 0 and n <= w.shape[2]
    in_specs = [pl.BlockSpec((tm, k), lambda i, j: (i, 0)),
                pl.BlockSpec((None, k, tn), lambda i, j: (layer, 0, j))]
    out_specs = pl.BlockSpec((tm, tn), lambda i, j: (i, j))
    out_shape = jax.ShapeDtypeStruct((m, n), out_dtype)
    if w_side is None:
        body, operands = _mm_kernel, (x, w)
    else:
        s = w_side.shape[2]
        body, operands = _mm_side_kernel, (x, w, w_side)
        in_specs.append(pl.BlockSpec((None, k, s), lambda i, j: (layer, 0, 0)))
        out_specs = (out_specs, pl.BlockSpec((tm, s), lambda i, j: (i, 0)))
        out_shape = (out_shape, jax.ShapeDtypeStruct((m, s), F32))
    return pl.pallas_call(
        body,
        grid=(m // tm, n // tn),
        in_specs=in_specs,
        out_specs=out_specs,
        out_shape=out_shape,
        compiler_params=_params("parallel", "arbitrary"),
        name="in_proj",
    )(*operands)


def _mm_ln_kernel(a_ref, w_ref, x_ref, g_ref, b_ref, o_ref, wbf_ref):
    @pl.when(pl.program_id(0) == 0)
    def _cast_weights():
        wbf_ref[...] = w_ref[...].astype(BF16)

    sub = a_ref.shape[0] // LN_ROW_SPLIT
    for r in range(LN_ROW_SPLIT):
        rs = slice(r * sub, (r + 1) * sub)
        m = jnp.dot(a_ref[rs, :], wbf_ref[...], preferred_element_type=F32)
        y = ALPHA * x_ref[rs, :] + m
        o_ref[rs, :] = _layer_norm(y, g_ref[...], b_ref[...])


def _matmul_res_ln(a, w, w_layer, x, g, b, ln_layer, tm):
    m, k = a.shape
    d = w.shape[2]
    return pl.pallas_call(
        _mm_ln_kernel,
        grid=(m // tm,),
        in_specs=[pl.BlockSpec((tm, k), lambda i: (i, 0)),
                  pl.BlockSpec((None, k, d), lambda i: (w_layer, 0, 0),
                               pipeline_mode=pl.Buffered(1)),
                  pl.BlockSpec((tm, d), lambda i: (i, 0)),
                  pl.BlockSpec((None, 1, d), lambda i: (ln_layer, 0, 0)),
                  pl.BlockSpec((None, 1, d), lambda i: (ln_layer, 0, 0))],
        out_specs=pl.BlockSpec((tm, d), lambda i: (i, 0)),
        out_shape=jax.ShapeDtypeStruct((m, d), F32),
        scratch_shapes=[pltpu.VMEM((k, d), BF16)],
        compiler_params=_params("arbitrary"),
        name="out_proj_ln",
    )(a, w, x, g, b)


def _gelu(a):
    return 0.5 * a * (1.0 + lax.erf(a * (2.0 ** -0.5)))


def _ffn_kernel(xp_ref, x_hbm, xn_ref, wa_ref, wb_ref, cw_ref, cb_ref, wd_ref, g_ref, b_ref,
                o_ref, xe_ref, xbuf_ref, x_sem, *, tm, tiles_per_seq):
    i = pl.program_id(0)
    j = pl.program_id(1)
    ni = pl.num_programs(0)
    nj = pl.num_programs(1)
    rows = tm + 2 * HALO
    tf = wd_ref.shape[0]
    d_ff = cw_ref.shape[1] // 2

    def x_copy(tile):
        return pltpu.make_async_copy(x_hbm.at[pl.ds(tile * tm, tm), :], xbuf_ref, x_sem)

    @pl.when(j == 0)
    def _stage_rows():
        @pl.when(i == 0)
        def _first_tile():
            x_copy(0).start()

        x_copy(i).wait()
        pos = i % tiles_per_seq
        keep_prev = (pos != 0).astype(F32)
        keep_next = (pos != tiles_per_seq - 1).astype(F32)
        zeros = jnp.zeros((HALO - 8, xbuf_ref.shape[1]), F32)
        prev = jnp.concatenate([zeros, xp_ref[...] * keep_prev], axis=0)
        nxt = jnp.concatenate([xn_ref[...] * keep_next, zeros], axis=0)
        xe_ref[0:HALO, :] = prev.astype(BF16)
        xe_ref[HALO:HALO + tm, :] = xbuf_ref[...].astype(BF16)
        xe_ref[HALO + tm:rows, :] = nxt.astype(BF16)
        o_ref[...] = ALPHA * xbuf_ref[...]

    @pl.when((j == 1) & (i + 1 < ni))
    def _prefetch_next_tile():
        x_copy(i + 1).start()

    xe = xe_ref[...]

    def conv_half(w_ref, col, p_col):
        h = jnp.dot(xe, w_ref[:, col:col + FFN_SUB], preferred_element_type=F32)
        h_prev = pltpu.roll(h, 1, 0)[HALO:HALO + tm]
        h_next = pltpu.roll(h, rows - 1, 0)[HALO:HALO + tm]
        ps = pl.ds(pl.multiple_of(p_col, FFN_SUB), FFN_SUB)
        cw = cw_ref[:, ps]
        return (h_prev * cw[0:1] + h[HALO:HALO + tm] * cw[1:2] + h_next * cw[2:3]
                + cb_ref[:, ps])

    contrib = None
    for c in range(tf // FFN_SUB):
        col = c * FFN_SUB
        a = conv_half(wa_ref, col, j * tf + col)
        bb = conv_half(wb_ref, col, d_ff + j * tf + col)
        u = (_gelu(a) * bb).astype(BF16)
        wd = wd_ref[col:col + FFN_SUB, :].astype(BF16)
        part = jnp.dot(u, wd, preferred_element_type=F32)
        contrib = part if contrib is None else contrib + part
    o_ref[...] += contrib

    @pl.when(j == nj - 1)
    def _finish():
        o_ref[...] = _layer_norm(o_ref[...], g_ref[...], b_ref[...])


def _conv_ffn_ln(x, w_up, conv_w, conv_b, w_down, g, b, layer, tm, tf):
    m, d = x.shape
    f = w_down.shape[1]
    nf = f // tf
    assert m % tm == 0 and f % tf == 0 and SEQ % tm == 0 and tm % 8 == 0
    assert tf % FFN_SUB == 0 and nf >= 2
    tb = tm // 8
    last_blk = m // 8 - 1
    kern = functools.partial(_ffn_kernel, tm=tm, tiles_per_seq=SEQ // tm)
    return pl.pallas_call(
        kern,
        grid=(m // tm, nf),
        in_specs=[
            pl.BlockSpec((8, d), lambda i, j: (jnp.maximum(i * tb - 1, 0), 0)),
            pl.BlockSpec(memory_space=pl.ANY),
            pl.BlockSpec((8, d), lambda i, j: (jnp.minimum((i + 1) * tb, last_blk), 0)),
            pl.BlockSpec((None, d, tf), lambda i, j: (layer, 0, j)),
            pl.BlockSpec((None, d, tf), lambda i, j: (layer, 0, j + nf)),
            pl.BlockSpec((None, 3, 2 * f), lambda i, j: (layer, 0, 0)),
            pl.BlockSpec((None, 1, 2 * f), lambda i, j: (layer, 0, 0)),
            pl.BlockSpec((None, tf, d), lambda i, j: (layer, j, 0)),
            pl.BlockSpec((None, 1, d), lambda i, j: (layer, 0, 0)),
            pl.BlockSpec((None, 1, d), lambda i, j: (layer, 0, 0)),
        ],
        out_specs=pl.BlockSpec((tm, d), lambda i, j: (i, 0), pipeline_mode=pl.Buffered(1)),
        out_shape=jax.ShapeDtypeStruct((m, d), F32),
        scratch_shapes=[pltpu.VMEM((tm + 2 * HALO, d), BF16),
                        pltpu.VMEM((tm, d), F32),
                        pltpu.SemaphoreType.DMA(())],
        compiler_params=_params("arbitrary", "arbitrary"),
        name="conv_ffn_ln",
    )(x, x, x, w_up, w_up, conv_w, conv_b, w_down, g, b)


def _bf16_terms(x, n_terms):
    terms = []
    for _ in range(n_terms - 1):
        t = x.astype(BF16)
        terms.append(t)
        x = x - t.astype(F32)
    terms.append(x.astype(BF16))
    return terms


def _split_dot(tri, x):
    t = tri.astype(BF16)
    hi, lo = _bf16_terms(x, 2)
    return (jnp.dot(t, hi, preferred_element_type=F32)
            + jnp.dot(t, lo, preferred_element_type=F32))


def _log_sigmoid(x):
    return jnp.minimum(x, 0.0) - jnp.log(1.0 + jnp.exp(-jnp.abs(x)))


def _gla_kernel(q_ref, k_ref, v_ref, g_ref, lr_ref, wg_ref, bg_ref, ng_ref, o_ref,
                la_ref, qg_ref, kg_ref, dg_ref, of_ref, ob_ref, sf_ref, sb_ref):
    t = q_ref.shape[0]
    c = GLA_CHUNK
    gp = GLA_GROUP
    gr = gp * c
    n_groups = t // gr
    scale = GLA_DK_HEAD ** -0.5
    nt = (((1,), (1,)), ((), ()))
    tn = (((0,), (0,)), ((), ()))

    lr = lr_ref[...]
    for z in range(2):
        logit = jnp.dot(lr[:, z * GLA_RANK:(z + 1) * GLA_RANK].astype(BF16),
                        wg_ref[z].astype(BF16), preferred_element_type=F32)
        la_ref[z] = _log_sigmoid(logit + bg_ref[z]) * (1.0 / GLA_TEMP)

    row = lax.broadcasted_iota(jnp.int32, (c, c), 0)
    col = lax.broadcasted_iota(jnp.int32, (c, c), 1)
    tri = ((row >= col).astype(F32), (row <= col).astype(F32))
    tot_row = (c - 1, 0)
    qrow = lax.broadcasted_iota(jnp.int32, (c, gr), 0)
    klane = lax.broadcasted_iota(jnp.int32, (c, gr), 1)
    s_refs = (sf_ref, sb_ref)
    o_refs = (of_ref, ob_ref)

    def sum_tots(tots, lo, hi):
        acc = jnp.zeros_like(tots[0])
        for p in range(lo, hi):
            acc = acc + tots[p]
        return acc

    def chunk_slices(g):
        return [pl.ds(pl.multiple_of(g * gr + p * c, c), c) for p in range(gp)]

    def local_cums(z, g):
        return [_split_dot(tri[z], la_ref[z, sl, :]) for sl in chunk_slices(g)]

    def local_scores(z, g, cums):
        sls = chunk_slices(g)
        tots = [cum[tot_row[z]:tot_row[z] + 1, :] for cum in cums]
        qd, ki, ks = [], [], []
        for sl, cum, tot in zip(sls, cums, tots):
            q = q_ref[sl, :].astype(F32) * scale
            k = k_ref[sl, :].astype(F32)
            qd.append(q * jnp.exp(cum))
            ki.append((k * jnp.exp(-cum)).astype(BF16))
            ks.append(k * jnp.exp(tot - cum))
        if z == 0:
            before = [sum_tots(tots, 0, p) for p in range(gp)]
            after = [sum_tots(tots, p + 1, gp) for p in range(gp)]
        else:
            before = [sum_tots(tots, p + 1, gp) for p in range(gp)]
            after = [sum_tots(tots, 0, p) for p in range(gp)]
        for p in range(gp):
            qg_ref[z, sls[p], :] = (qd[p] * jnp.exp(before[p])).astype(BF16)
            kg_ref[z, sls[p], :] = (ks[p] * jnp.exp(after[p])).astype(BF16)
        dg_ref[z, pl.ds(g, 1), :] = jnp.exp(sum_tots(tots, 0, gp))
        scores = []
        for pq in range(gp):
            keys = []
            for pk in range(gp):
                seen = pk < pq if z == 0 else pk > pq
                if pk == pq:
                    keys.append(ki[pk])
                elif not seen:
                    keys.append(jnp.zeros_like(ki[pk]))
                else:
                    lo, hi = (pk + 1, pq) if z == 0 else (pq + 1, pk)
                    keys.append((ks[pk] * jnp.exp(sum_tots(tots, lo, hi))).astype(BF16))
            scores.append(lax.dot_general(qd[pq].astype(BF16), jnp.concatenate(keys, axis=0),
                                          nt, preferred_element_type=F32))
        return scores

    def local_out(z, g, scores):
        probs = []
        for pq, s in enumerate(scores):
            visible = (klane <= qrow + pq * c) if z == 0 else (klane > qrow + pq * c)
            probs.append(jnp.where(visible, s, 0.0).astype(BF16))
        rows = pl.ds(pl.multiple_of(g * gr, gr), gr)
        o_refs[z][rows, :] = jnp.dot(jnp.concatenate(probs, axis=0), v_ref[rows, :],
                                     preferred_element_type=F32)

    def local_body(it, carry):
        work = [(z, it * GLA_LOCAL_GROUPS + u) for u in range(GLA_LOCAL_GROUPS) for z in range(2)]
        cums = [local_cums(z, g) for z, g in work]
        scores = [local_scores(z, g, cum) for (z, g), cum in zip(work, cums)]
        for (z, g), s in zip(work, scores):
            local_out(z, g, s)
        return carry

    lax.fori_loop(0, n_groups // GLA_LOCAL_GROUPS, local_body, 0)

    sf_ref[...] = jnp.zeros_like(sf_ref)
    sb_ref[...] = jnp.zeros_like(sb_ref)

    def state_body(it, carry):
        steps = []
        for z in range(2):
            for u in range(GLA_STATE_GROUP):
                i = it * GLA_STATE_GROUP + u
                g = i if z == 0 else n_groups - 1 - i
                rows = pl.ds(pl.multiple_of(g * gr, gr), gr)
                upd = lax.dot_general(v_ref[rows, :], kg_ref[z, rows, :], tn,
                                      preferred_element_type=F32)
                steps.append((z, g, rows, upd))
        st = [sf_ref[...], sb_ref[...]]
        for z, g, rows, upd in steps:
            o_refs[z][rows, :] += lax.dot_general(qg_ref[z, rows, :], st[z].astype(BF16), nt,
                                                  preferred_element_type=F32)
            st[z] = st[z] * dg_ref[z, pl.ds(g, 1), :] + upd
        sf_ref[...] = st[0]
        sb_ref[...] = st[1]
        return carry

    lax.fori_loop(0, n_groups // GLA_STATE_GROUP, state_body, 0)

    o = of_ref[...] + ob_ref[...]
    o = o * lax.rsqrt(jnp.mean(o * o, axis=-1, keepdims=True) + RMS_EPS) * ng_ref[...]
    gate = g_ref[...].astype(F32)
    o_ref[...] = (o * (gate * jax.nn.sigmoid(gate))).astype(o_ref.dtype)


def _gla_core(p, lr, w_gate, b_gate, norm_g, layer, batch):
    m = p.shape[0]
    t = m // batch
    h, dk, dv = GLA_HEADS, GLA_DK_HEAD, GLA_DV_HEAD
    return pl.pallas_call(
        _gla_kernel,
        grid=(batch, h),
        in_specs=[
            pl.BlockSpec((t, dk), lambda b, i: (b, i)),
            pl.BlockSpec((t, dk), lambda b, i: (b, h + i)),
            pl.BlockSpec((t, dv), lambda b, i: (b, h + i)),
            pl.BlockSpec((t, dv), lambda b, i: (b, 2 * h + i)),
            pl.BlockSpec((t, 2 * GLA_RANK), lambda b, i: (b, 0)),
            pl.BlockSpec((None, 2, GLA_RANK, dk), lambda b, i: (layer, 0, 0, i)),
            pl.BlockSpec((None, 2, 1, dk), lambda b, i: (layer, 0, 0, i)),
            pl.BlockSpec((None, 1, dv), lambda b, i: (layer, 0, 0)),
        ],
        out_specs=pl.BlockSpec((t, dv), lambda b, i: (b, i)),
        out_shape=jax.ShapeDtypeStruct((m, GLA_DV), BF16),
        scratch_shapes=[pltpu.VMEM((2, t, dk), F32),
                        pltpu.VMEM((2, t, dk), BF16), pltpu.VMEM((2, t, dk), BF16),
                        pltpu.VMEM((2, t // (GLA_CHUNK * GLA_GROUP), dk), F32),
                        pltpu.VMEM((t, dv), F32), pltpu.VMEM((t, dv), F32),
                        pltpu.VMEM((dv, dk), F32), pltpu.VMEM((dv, dk), F32)],
        compiler_params=_params("parallel", "arbitrary"),
        name="gla_core",
    )(p, p, p, p, lr, w_gate, b_gate, norm_g)


NA_DR = 2 * NA_KR
NA_DC = 2 * NA_KC - 1
NA_PAIR_W = 2 * GRID_W


def _na_col_geometry(shape):
    log_w = GRID_W.bit_length() - 1
    assert 1 << log_w == GRID_W
    col = lax.broadcasted_iota(jnp.int32, shape, 1)
    qc = col >> (log_w + 1)
    half = (col >> log_w) & 1
    kc = col & (GRID_W - 1)
    win = jnp.clip(qc - NA_KC // 2, 0, GRID_W - NA_KC)
    valid = (kc >= win) & (kc < win + NA_KC)
    src = half * NA_DC + (kc - qc + NA_KC - 1)
    return src, valid


def _na_bias_kernel(rpb2_ref, o_ref, sel_ref, valid_ref):
    n = o_ref.shape[1]

    @pl.when((pl.program_id(0) == 0) & (pl.program_id(1) == 0))
    def _build_selector():
        src, valid = _na_col_geometry((GRID_W, n))
        r = lax.broadcasted_iota(jnp.int32, (GRID_W, n), 0)
        sel_ref[...] = jnp.where(valid & (r == src), 1.0, 0.0).astype(BF16)
        _, valid_o = _na_col_geometry((NA_DR, n))
        valid_ref[...] = jnp.where(valid_o, 1.0, 0.0)

    sel = sel_ref[...]
    acc = jnp.zeros(o_ref.shape, F32)
    for term in _bf16_terms(rpb2_ref[...], 3):
        acc = acc + jnp.dot(term, sel, preferred_element_type=F32)
    o_ref[...] = jnp.where(valid_ref[...] > 0.5, acc, NEG_INF) * LOG2E


def _na_bias_tiles(rpb):
    nl, nh = rpb.shape[:2]
    assert rpb.shape[2] == NA_DR - 1 and rpb.shape[3] == NA_DC and 2 * NA_DC <= GRID_W
    rp = jnp.pad(rpb.astype(F32), ((0, 0), (0, 0), (0, 2), (0, 0)))
    pad = jnp.zeros((nl, nh, NA_DR, GRID_W - 2 * NA_DC), F32)
    rpb2 = jnp.concatenate([rp[:, :, :NA_DR], rp[:, :, 1:NA_DR + 1], pad], axis=-1)
    n = GRID_W * NA_PAIR_W
    tiles = pl.pallas_call(
        _na_bias_kernel,
        grid=(nl, nh),
        in_specs=[pl.BlockSpec((None, None, NA_DR, GRID_W), lambda l, h: (l, h, 0, 0))],
        out_specs=pl.BlockSpec((None, None, NA_DR, n), lambda l, h: (l, h, 0, 0)),
        out_shape=jax.ShapeDtypeStruct((nl, nh, NA_DR, n), F32),
        scratch_shapes=[pltpu.VMEM((GRID_W, n), BF16), pltpu.VMEM((NA_DR, n), F32)],
        compiler_params=_params("arbitrary", "arbitrary"),
        name="na_bias",
    )(rpb2)
    return tiles.reshape(nl, nh, NA_DR, GRID_W, NA_PAIR_W)


def _na_kernel(q_ref, k_ref, v_ref, bias_ref, o_ref):
    rows = q_ref.shape[0] // GRID_W
    band = NA_KR * GRID_W
    scale = NA_HD ** -0.5 * LOG2E

    def head_body(hh, it):
        hs = slice(hh * NA_HD, (hh + 1) * NA_HD)
        group = []
        for u in range(NA_ROW_GROUP):
            r = it * NA_ROW_GROUP + u
            rs = jnp.clip(r - NA_KR // 2, 0, rows - NA_KR)
            d0 = rs - r + NA_KR - 1
            qs = pl.ds(pl.multiple_of(r * GRID_W, GRID_W), GRID_W)
            ks = pl.ds(pl.multiple_of(rs * GRID_W, GRID_W), band)
            s = lax.dot_general(q_ref[qs, hs], k_ref[ks, hs], (((1,), (1,)), ((), ())),
                                preferred_element_type=F32)
            group.append((qs, ks, d0, s))
        probs = []
        for qs, ks, d0, s in group:
            bias = jnp.concatenate([bias_ref[hh, d0 + 2 * p] for p in range(NA_KR // 2)],
                                   axis=1)
            s = s * scale + bias
            p = jnp.exp2(s - jnp.max(s, axis=-1, keepdims=True))
            l = jnp.sum(p, axis=-1, keepdims=True)
            probs.append((qs, ks, p.astype(BF16), l))
        for qs, ks, p, l in probs:
            o = jnp.dot(p, v_ref[ks, hs], preferred_element_type=F32)
            o_ref[qs, hs] = (o / l).astype(o_ref.dtype)

    for hh in range(NA_HEADS_PER_STEP):
        def body(it, carry, hh=hh):
            head_body(hh, it)
            return carry

        lax.fori_loop(0, rows // NA_ROW_GROUP, body, 0)


def _na_core(qkv, bias, layer, batch):
    m = qkv.shape[0]
    t = m // batch
    hps = NA_HEADS_PER_STEP
    assert NA_HEADS % hps == 0
    nb, w = NA_HEADS // hps, hps * NA_HD
    return pl.pallas_call(
        _na_kernel,
        grid=(batch, nb),
        in_specs=[
            pl.BlockSpec((t, w), lambda b, i: (b, i)),
            pl.BlockSpec((t, w), lambda b, i: (b, nb + i)),
            pl.BlockSpec((t, w), lambda b, i: (b, 2 * nb + i)),
            pl.BlockSpec((None, hps, NA_DR, GRID_W, NA_PAIR_W),
                         lambda b, i: (layer, i, 0, 0, 0)),
        ],
        out_specs=pl.BlockSpec((t, w), lambda b, i: (b, i)),
        out_shape=jax.ShapeDtypeStruct((m, D_MODEL), BF16),
        compiler_params=_params("parallel", "arbitrary"),
        name="na_core",
    )(qkv, qkv, qkv, bias)


def kernel(x, gla_w_in, gla_w_gate_up, gla_b_gate, gla_norm_g, gla_w_out, na_w_in, na_rpb, na_w_out, ffn_w_up, ffn_conv_w, ffn_conv_b, ffn_w_down, ln_mix_g, ln_mix_b, ln_ffn_g, ln_ffn_b):
    batch, seq, d = x.shape
    assert seq == SEQ and d == D_MODEL
    n_main = 2 * GLA_DK + 2 * GLA_DV
    gla_w = gla_w_in[:, :, :n_main].astype(BF16)
    gla_w_lr = gla_w_in[:, :, n_main:].astype(BF16)
    gla_bg = gla_b_gate.reshape(gla_b_gate.shape[0], 2, 1, GLA_DK)
    gla_ng = gla_norm_g.reshape(gla_norm_g.shape[0], 1, GLA_DV_HEAD)
    na_w = na_w_in.astype(BF16)
    na_bias = _na_bias_tiles(na_rpb)
    w_up = ffn_w_up.astype(BF16)
    conv_b = ffn_conv_b.reshape(DEPTH, 1, 2 * D_FF)
    mix_g, mix_b = ln_mix_g.reshape(DEPTH, 1, d), ln_mix_b.reshape(DEPTH, 1, d)
    ffn_g, ffn_b = ln_ffn_g.reshape(DEPTH, 1, d), ln_ffn_b.reshape(DEPTH, 1, d)

    xf = x.reshape(batch * seq, d)
    for i in range(DEPTH):
        j = i // 2
        if i % 2 == 0:
            p, lr = _matmul(xf, gla_w, j, n_main, BF16, tm=PROJ_TM, tn=PROJ_TN, w_side=gla_w_lr)
            mix = _gla_core(p, lr, gla_w_gate_up, gla_bg, gla_ng, j, batch)
            w_out = gla_w_out
        else:
            qkv = _matmul(xf, na_w, j, 3 * d, BF16, tm=PROJ_TM, tn=PROJ_TN)
            mix = _na_core(qkv, na_bias, j, batch)
            w_out = na_w_out
        xf = _matmul_res_ln(mix, w_out, j, xf, mix_g, mix_b, i, tm=OUT_TM)
        xf = _conv_ffn_ln(xf, w_up, ffn_conv_w, conv_b, ffn_w_down, ffn_g, ffn_b, i,
                          tm=FFN_TM, tf=FFN_TF)
    return xf.reshape(batch, seq, d)
```

```python
import functools
import math

import jax
import jax.numpy as jnp
import numpy as np
from jax import lax
from jax.experimental import pallas as pl
from jax.experimental.pallas import tpu as pltpu

F32 = jnp.float32
BF16 = jnp.bfloat16

D_MODEL = 2048
SEQ = 2048
DEPTH = 4
GRID_W = 64
GLA_HEADS = 4
GLA_DK = D_MODEL // 2
GLA_DV = D_MODEL
GLA_DK_HEAD = GLA_DK // GLA_HEADS
GLA_DV_HEAD = GLA_DV // GLA_HEADS
GLA_RANK = 16
GLA_TEMP = 16.0
GLA_CHUNK = 64
GLA_GROUP = 4
GLA_LOCAL_GROUPS = 4
GLA_STATE_GROUP = 4
NA_HEADS = 16
NA_HD = D_MODEL // NA_HEADS
NA_KR = 8
NA_KC = 16
NA_HEADS_PER_STEP = 1
NA_ROW_GROUP = 32
D_FF = 11 * D_MODEL // 4
LN_EPS = 1e-5
RMS_EPS = 1e-6
NEG_INF = -1e9
ALPHA = (2.0 * DEPTH) ** 0.25
LOG2E = math.log2(math.e)

V7X_VMEM_BYTES = 64 * 1024 * 1024
VMEM_LIMIT = V7X_VMEM_BYTES - 8 * 1024 * 1024
LN_ROW_SPLIT = 4
OUT_TM = 512
PROJ_TM = 1024
PROJ_TN = 2048
FFN_TM = 1024
FFN_TF = 512
FFN_SUB = 256
HALO = 16


def _params(*sem):
    return pltpu.CompilerParams(dimension_semantics=sem, vmem_limit_bytes=VMEM_LIMIT)


def _layer_norm(y, g, b):
    mu = jnp.mean(y, axis=-1, keepdims=True)
    yc = y - mu
    var = jnp.mean(yc * yc, axis=-1, keepdims=True)
    return yc * lax.rsqrt(var + LN_EPS) * g + b


def _mm_kernel(x_ref, w_ref, o_ref):
    x = x_ref[...].astype(BF16)
    o_ref[...] = jnp.dot(x, w_ref[...], preferred_element_type=F32).astype(o_ref.dtype)


def _mm_side_kernel(x_ref, w_ref, ws_ref, o_ref, os_ref):
    x = x_ref[...].astype(BF16)
    o_ref[...] = jnp.dot(x, w_ref[...], preferred_element_type=F32).astype(o_ref.dtype)

    @pl.when(pl.program_id(1) == 0)
    def _side():
        os_ref[...] = jnp.dot(x, ws_ref[...], preferred_element_type=F32)


def _matmul(x, w, layer, n, out_dtype, tm, tn, w_side=None):
    m, k = x.shape
    assert m % tm == 0 and n % tn == 0 and n <= w.shape[2]
    in_specs = [pl.BlockSpec((tm, k), lambda i, j: (i, 0)),
                pl.BlockSpec((None, k, tn), lambda i, j: (layer, 0, j))]
    out_specs = pl.BlockSpec((tm, tn), lambda i, j: (i, j))
    out_shape = jax.ShapeDtypeStruct((m, n), out_dtype)
    if w_side is None:
        body, operands = _mm_kernel, (x, w)
    else:
        s = w_side.shape[2]
        body, operands = _mm_side_kernel, (x, w, w_side)
        in_specs.append(pl.BlockSpec((None, k, s), lambda i, j: (layer, 0, 0)))
        out_specs = (out_specs, pl.BlockSpec((tm, s), lambda i, j: (i, 0)))
        out_shape = (out_shape, jax.ShapeDtypeStruct((m, s), F32))
    return pl.pallas_call(
        body,
        grid=(m // tm, n // tn),
        in_specs=in_specs,
        out_specs=out_specs,
        out_shape=out_shape,
        compiler_params=_params("parallel", "arbitrary"),
        name="in_proj",
    )(*operands)


def _mm_ln_kernel(a_ref, w_ref, x_ref, g_ref, b_ref, o_ref, wbf_ref):
    @pl.when(pl.program_id(0) == 0)
    def _cast_weights():
        wbf_ref[...] = w_ref[...].astype(BF16)

    sub = a_ref.shape[0] // LN_ROW_SPLIT
    for r in range(LN_ROW_SPLIT):
        rs = slice(r * sub, (r + 1) * sub)
        m = jnp.dot(a_ref[rs, :], wbf_ref[...], preferred_element_type=F32)
        y = ALPHA * x_ref[rs, :] + m
        o_ref[rs, :] = _layer_norm(y, g_ref[...], b_ref[...])


def _matmul_res_ln(a, w, w_layer, x, g, b, ln_layer, tm):
    m, k = a.shape
    d = w.shape[2]
    return pl.pallas_call(
        _mm_ln_kernel,
        grid=(m // tm,),
        in_specs=[pl.BlockSpec((tm, k), lambda i: (i, 0)),
                  pl.BlockSpec((None, k, d), lambda i: (w_layer, 0, 0),
                               pipeline_mode=pl.Buffered(1)),
                  pl.BlockSpec((tm, d), lambda i: (i, 0)),
                  pl.BlockSpec((None, 1, d), lambda i: (ln_layer, 0, 0)),
                  pl.BlockSpec((None, 1, d), lambda i: (ln_layer, 0, 0))],
        out_specs=pl.BlockSpec((tm, d), lambda i: (i, 0)),
        out_shape=jax.ShapeDtypeStruct((m, d), F32),
        scratch_shapes=[pltpu.VMEM((k, d), BF16)],
        compiler_params=_params("arbitrary"),
        name="out_proj_ln",
    )(a, w, x, g, b)


def _gelu(a):
    return 0.5 * a * (1.0 + lax.erf(a * (2.0 ** -0.5)))


def _ffn_kernel(xp_ref, x_hbm, xn_ref, wa_ref, wb_ref, cw_ref, cb_ref, wd_ref, g_ref, b_ref,
                o_ref, xe_ref, xbuf_ref, x_sem, *, tm, tiles_per_seq):
    i = pl.program_id(0)
    j = pl.program_id(1)
    ni = pl.num_programs(0)
    nj = pl.num_programs(1)
    rows = tm + 2 * HALO
    tf = wd_ref.shape[0]
    d_ff = cw_ref.shape[1] // 2

    def x_copy(tile):
        return pltpu.make_async_copy(x_hbm.at[pl.ds(tile * tm, tm), :], xbuf_ref, x_sem)

    @pl.when(j == 0)
    def _stage_rows():
        @pl.when(i == 0)
        def _first_tile():
            x_copy(0).start()

        x_copy(i).wait()
        pos = i % tiles_per_seq
        keep_prev = (pos != 0).astype(F32)
        keep_next = (pos != tiles_per_seq - 1).astype(F32)
        zeros = jnp.zeros((HALO - 8, xbuf_ref.shape[1]), F32)
        prev = jnp.concatenate([zeros, xp_ref[...] * keep_prev], axis=0)
        nxt = jnp.concatenate([xn_ref[...] * keep_next, zeros], axis=0)
        xe_ref[0:HALO, :] = prev.astype(BF16)
        xe_ref[HALO:HALO + tm, :] = xbuf_ref[...].astype(BF16)
        xe_ref[HALO + tm:rows, :] = nxt.astype(BF16)
        o_ref[...] = ALPHA * xbuf_ref[...]

    @pl.when((j == 1) & (i + 1 < ni))
    def _prefetch_next_tile():
        x_copy(i + 1).start()

    xe = xe_ref[...]

    def conv_half(w_ref, col, p_col):
        h = jnp.dot(xe, w_ref[:, col:col + FFN_SUB], preferred_element_type=F32)
        h_prev = pltpu.roll(h, 1, 0)[HALO:HALO + tm]
        h_next = pltpu.roll(h, rows - 1, 0)[HALO:HALO + tm]
        ps = pl.ds(pl.multiple_of(p_col, FFN_SUB), FFN_SUB)
        cw = cw_ref[:, ps]
        return (h_prev * cw[0:1] + h[HALO:HALO + tm] * cw[1:2] + h_next * cw[2:3]
                + cb_ref[:, ps])

    contrib = None
    for c in range(tf // FFN_SUB):
        col = c * FFN_SUB
        a = conv_half(wa_ref, col, j * tf + col)
        bb = conv_half(wb_ref, col, d_ff + j * tf + col)
        u = (_gelu(a) * bb).astype(BF16)
        wd = wd_ref[col:col + FFN_SUB, :].astype(BF16)
        part = jnp.dot(u, wd, preferred_element_type=F32)
        contrib = part if contrib is None else contrib + part
    o_ref[...] += contrib

    @pl.when(j == nj - 1)
    def _finish():
        o_ref[...] = _layer_norm(o_ref[...], g_ref[...], b_ref[...])


def _conv_ffn_ln(x, w_up, conv_w, conv_b, w_down, g, b, layer, tm, tf):
    m, d = x.shape
    f = w_down.shape[1]
    nf = f // tf
    assert m % tm == 0 and f % tf == 0 and SEQ % tm == 0 and tm % 8 == 0
    assert tf % FFN_SUB == 0 and nf >= 2
    tb = tm // 8
    last_blk = m // 8 - 1
    kern = functools.partial(_ffn_kernel, tm=tm, tiles_per_seq=SEQ // tm)
    return pl.pallas_call(
        kern,
        grid=(m // tm, nf),
        in_specs=[
            pl.BlockSpec((8, d), lambda i, j: (jnp.maximum(i * tb - 1, 0), 0)),
            pl.BlockSpec(memory_space=pl.ANY),
            pl.BlockSpec((8, d), lambda i, j: (jnp.minimum((i + 1) * tb, last_blk), 0)),
            pl.BlockSpec((None, d, tf), lambda i, j: (layer, 0, j)),
            pl.BlockSpec((None, d, tf), lambda i, j: (layer, 0, j + nf)),
            pl.BlockSpec((None, 3, 2 * f), lambda i, j: (layer, 0, 0)),
            pl.BlockSpec((None, 1, 2 * f), lambda i, j: (layer, 0, 0)),
            pl.BlockSpec((None, tf, d), lambda i, j: (layer, j, 0)),
            pl.BlockSpec((None, 1, d), lambda i, j: (layer, 0, 0)),
            pl.BlockSpec((None, 1, d), lambda i, j: (layer, 0, 0)),
        ],
        out_specs=pl.BlockSpec((tm, d), lambda i, j: (i, 0), pipeline_mode=pl.Buffered(1)),
        out_shape=jax.ShapeDtypeStruct((m, d), F32),
        scratch_shapes=[pltpu.VMEM((tm + 2 * HALO, d), BF16),
                        pltpu.VMEM((tm, d), F32),
                        pltpu.SemaphoreType.DMA(())],
        compiler_params=_params("arbitrary", "arbitrary"),
        name="conv_ffn_ln",
    )(x, x, x, w_up, w_up, conv_w, conv_b, w_down, g, b)


def _bf16_terms(x, n_terms):
    terms = []
    for _ in range(n_terms - 1):
        t = x.astype(BF16)
        terms.append(t)
        x = x - t.astype(F32)
    terms.append(x.astype(BF16))
    return terms


def _split_dot(tri, x):
    t = tri.astype(BF16)
    hi, lo = _bf16_terms(x, 2)
    return (jnp.dot(t, hi, preferred_element_type=F32)
            + jnp.dot(t, lo, preferred_element_type=F32))


def _log_sigmoid(x):
    return jnp.minimum(x, 0.0) - jnp.log(1.0 + jnp.exp(-jnp.abs(x)))


def _gla_kernel(q_ref, k_ref, v_ref, g_ref, lr_ref, wg_ref, bg_ref, ng_ref, o_ref,
                la_ref, qg_ref, kg_ref, dg_ref, of_ref, ob_ref, sf_ref, sb_ref):
    t = q_ref.shape[0]
    c = GLA_CHUNK
    gp = GLA_GROUP
    gr = gp * c
    n_groups = t // gr
    scale = GLA_DK_HEAD ** -0.5
    nt = (((1,), (1,)), ((), ()))
    tn = (((0,), (0,)), ((), ()))

    lr = lr_ref[...]
    for z in range(2):
        logit = jnp.dot(lr[:, z * GLA_RANK:(z + 1) * GLA_RANK].astype(BF16),
                        wg_ref[z].astype(BF16), preferred_element_type=F32)
        la_ref[z] = _log_sigmoid(logit + bg_ref[z]) * (1.0 / GLA_TEMP)

    row = lax.broadcasted_iota(jnp.int32, (c, c), 0)
    col = lax.broadcasted_iota(jnp.int32, (c, c), 1)
    tri = ((row >= col).astype(F32), (row <= col).astype(F32))
    tot_row = (c - 1, 0)
    qrow = lax.broadcasted_iota(jnp.int32, (c, gr), 0)
    klane = lax.broadcasted_iota(jnp.int32, (c, gr), 1)
    s_refs = (sf_ref, sb_ref)
    o_refs = (of_ref, ob_ref)

    def sum_tots(tots, lo, hi):
        acc = jnp.zeros_like(tots[0])
        for p in range(lo, hi):
            acc = acc + tots[p]
        return acc

    def chunk_slices(g):
        return [pl.ds(pl.multiple_of(g * gr + p * c, c), c) for p in range(gp)]

    def local_cums(z, g):
        return [_split_dot(tri[z], la_ref[z, sl, :]) for sl in chunk_slices(g)]

    def local_scores(z, g, cums):
        sls = chunk_slices(g)
        tots = [cum[tot_row[z]:tot_row[z] + 1, :] for cum in cums]
        qd, ki, ks = [], [], []
        for sl, cum, tot in zip(sls, cums, tots):
            q = q_ref[sl, :].astype(F32) * scale
            k = k_ref[sl, :].astype(F32)
            qd.append(q * jnp.exp(cum))
            ki.append((k * jnp.exp(-cum)).astype(BF16))
            ks.append(k * jnp.exp(tot - cum))
        if z == 0:
            before = [sum_tots(tots, 0, p) for p in range(gp)]
            after = [sum_tots(tots, p + 1, gp) for p in range(gp)]
        else:
            before = [sum_tots(tots, p + 1, gp) for p in range(gp)]
            after = [sum_tots(tots, 0, p) for p in range(gp)]
        for p in range(gp):
            qg_ref[z, sls[p], :] = (qd[p] * jnp.exp(before[p])).astype(BF16)
            kg_ref[z, sls[p], :] = (ks[p] * jnp.exp(after[p])).astype(BF16)
        dg_ref[z, pl.ds(g, 1), :] = jnp.exp(sum_tots(tots, 0, gp))
        scores = []
        for pq in range(gp):
            keys = []
            for pk in range(gp):
                seen = pk < pq if z == 0 else pk > pq
                if pk == pq:
                    keys.append(ki[pk])
                elif not seen:
                    keys.append(jnp.zeros_like(ki[pk]))
                else:
                    lo, hi = (pk + 1, pq) if z == 0 else (pq + 1, pk)
                    keys.append((ks[pk] * jnp.exp(sum_tots(tots, lo, hi))).astype(BF16))
            scores.append(lax.dot_general(qd[pq].astype(BF16), jnp.concatenate(keys, axis=0),
                                          nt, preferred_element_type=F32))
        return scores

    def local_out(z, g, scores):
        probs = []
        for pq, s in enumerate(scores):
            visible = (klane <= qrow + pq * c) if z == 0 else (klane > qrow + pq * c)
            probs.append(jnp.where(visible, s, 0.0).astype(BF16))
        rows = pl.ds(pl.multiple_of(g * gr, gr), gr)
        o_refs[z][rows, :] = jnp.dot(jnp.concatenate(probs, axis=0), v_ref[rows, :],
                                     preferred_element_type=F32)

    def local_body(it, carry):
        work = [(z, it * GLA_LOCAL_GROUPS + u) for u in range(GLA_LOCAL_GROUPS) for z in range(2)]
        cums = [local_cums(z, g) for z, g in work]
        scores = [local_scores(z, g, cum) for (z, g), cum in zip(work, cums)]
        for (z, g), s in zip(work, scores):
            local_out(z, g, s)
        return carry

    lax.fori_loop(0, n_groups // GLA_LOCAL_GROUPS, local_body, 0)

    sf_ref[...] = jnp.zeros_like(sf_ref)
    sb_ref[...] = jnp.zeros_like(sb_ref)

    def state_body(it, carry):
        steps = []
        for z in range(2):
            for u in range(GLA_STATE_GROUP):
                i = it * GLA_STATE_GROUP + u
                g = i if z == 0 else n_groups - 1 - i
                rows = pl.ds(pl.multiple_of(g * gr, gr), gr)
                upd = lax.dot_general(v_ref[rows, :], kg_ref[z, rows, :], tn,
                                      preferred_element_type=F32)
                steps.append((z, g, rows, upd))
        st = [sf_ref[...], sb_ref[...]]
        for z, g, rows, upd in steps:
            o_refs[z][rows, :] += lax.dot_general(qg_ref[z, rows, :], st[z].astype(BF16), nt,
                                                  preferred_element_type=F32)
            st[z] = st[z] * dg_ref[z, pl.ds(g, 1), :] + upd
        sf_ref[...] = st[0]
        sb_ref[...] = st[1]
        return carry

    lax.fori_loop(0, n_groups // GLA_STATE_GROUP, state_body, 0)

    o = of_ref[...] + ob_ref[...]
    o = o * lax.rsqrt(jnp.mean(o * o, axis=-1, keepdims=True) + RMS_EPS) * ng_ref[...]
    gate = g_ref[...].astype(F32)
    o_ref[...] = (o * (gate * jax.nn.sigmoid(gate))).astype(o_ref.dtype)


def _gla_core(p, lr, w_gate, b_gate, norm_g, layer, batch):
    m = p.shape[0]
    t = m // batch
    h, dk, dv = GLA_HEADS, GLA_DK_HEAD, GLA_DV_HEAD
    return pl.pallas_call(
        _gla_kernel,
        grid=(batch, h),
        in_specs=[
            pl.BlockSpec((t, dk), lambda b, i: (b, i)),
            pl.BlockSpec((t, dk), lambda b, i: (b, h + i)),
            pl.BlockSpec((t, dv), lambda b, i: (b, h + i)),
            pl.BlockSpec((t, dv), lambda b, i: (b, 2 * h + i)),
            pl.BlockSpec((t, 2 * GLA_RANK), lambda b, i: (b, 0)),
            pl.BlockSpec((None, 2, GLA_RANK, dk), lambda b, i: (layer, 0, 0, i)),
            pl.BlockSpec((None, 2, 1, dk), lambda b, i: (layer, 0, 0, i)),
            pl.BlockSpec((None, 1, dv), lambda b, i: (layer, 0, 0)),
        ],
        out_specs=pl.BlockSpec((t, dv), lambda b, i: (b, i)),
        out_shape=jax.ShapeDtypeStruct((m, GLA_DV), BF16),
        scratch_shapes=[pltpu.VMEM((2, t, dk), F32),
                        pltpu.VMEM((2, t, dk), BF16), pltpu.VMEM((2, t, dk), BF16),
                        pltpu.VMEM((2, t // (GLA_CHUNK * GLA_GROUP), dk), F32),
                        pltpu.VMEM((t, dv), F32), pltpu.VMEM((t, dv), F32),
                        pltpu.VMEM((dv, dk), F32), pltpu.VMEM((dv, dk), F32)],
        compiler_params=_params("parallel", "arbitrary"),
        name="gla_core",
    )(p, p, p, p, lr, w_gate, b_gate, norm_g)


NA_DR = 2 * NA_KR
NA_DC = 2 * NA_KC - 1
NA_PAIR_W = 2 * GRID_W


def _na_col_geometry(shape):
    log_w = GRID_W.bit_length() - 1
    assert 1 << log_w == GRID_W
    col = lax.broadcasted_iota(jnp.int32, shape, 1)
    qc = col >> (log_w + 1)
    half = (col >> log_w) & 1
    kc = col & (GRID_W - 1)
    win = jnp.clip(qc - NA_KC // 2, 0, GRID_W - NA_KC)
    valid = (kc >= win) & (kc < win + NA_KC)
    src = half * NA_DC + (kc - qc + NA_KC - 1)
    return src, valid


def _na_bias_kernel(rpb2_ref, o_ref, sel_ref, valid_ref):
    n = o_ref.shape[1]

    @pl.when((pl.program_id(0) == 0) & (pl.program_id(1) == 0))
    def _build_selector():
        src, valid = _na_col_geometry((GRID_W, n))
        r = lax.broadcasted_iota(jnp.int32, (GRID_W, n), 0)
        sel_ref[...] = jnp.where(valid & (r == src), 1.0, 0.0).astype(BF16)
        _, valid_o = _na_col_geometry((NA_DR, n))
        valid_ref[...] = jnp.where(valid_o, 1.0, 0.0)

    sel = sel_ref[...]
    acc = jnp.zeros(o_ref.shape, F32)
    for term in _bf16_terms(rpb2_ref[...], 3):
        acc = acc + jnp.dot(term, sel, preferred_element_type=F32)
    o_ref[...] = jnp.where(valid_ref[...] > 0.5, acc, NEG_INF) * LOG2E


def _na_bias_tiles(rpb):
    nl, nh = rpb.shape[:2]
    assert rpb.shape[2] == NA_DR - 1 and rpb.shape[3] == NA_DC and 2 * NA_DC <= GRID_W
    rp = jnp.pad(rpb.astype(F32), ((0, 0), (0, 0), (0, 2), (0, 0)))
    pad = jnp.zeros((nl, nh, NA_DR, GRID_W - 2 * NA_DC), F32)
    rpb2 = jnp.concatenate([rp[:, :, :NA_DR], rp[:, :, 1:NA_DR + 1], pad], axis=-1)
    n = GRID_W * NA_PAIR_W
    tiles = pl.pallas_call(
        _na_bias_kernel,
        grid=(nl, nh),
        in_specs=[pl.BlockSpec((None, None, NA_DR, GRID_W), lambda l, h: (l, h, 0, 0))],
        out_specs=pl.BlockSpec((None, None, NA_DR, n), lambda l, h: (l, h, 0, 0)),
        out_shape=jax.ShapeDtypeStruct((nl, nh, NA_DR, n), F32),
        scratch_shapes=[pltpu.VMEM((GRID_W, n), BF16), pltpu.VMEM((NA_DR, n), F32)],
        compiler_params=_params("arbitrary", "arbitrary"),
        name="na_bias",
    )(rpb2)
    return tiles.reshape(nl, nh, NA_DR, GRID_W, NA_PAIR_W)


def _na_kernel(q_ref, k_ref, v_ref, bias_ref, o_ref):
    rows = q_ref.shape[0] // GRID_W
    band = NA_KR * GRID_W
    scale = NA_HD ** -0.5 * LOG2E

    def head_body(hh, it):
        hs = slice(hh * NA_HD, (hh + 1) * NA_HD)
        group = []
        for u in range(NA_ROW_GROUP):
            r = it * NA_ROW_GROUP + u
            if isinstance(r, int):
                rs = min(max(r - NA_KR // 2, 0), rows - NA_KR)
                qs, ks = pl.ds(r * GRID_W, GRID_W), pl.ds(rs * GRID_W, band)
            else:
                rs = jnp.clip(r - NA_KR // 2, 0, rows - NA_KR)
                qs = pl.ds(pl.multiple_of(r * GRID_W, GRID_W), GRID_W)
                ks = pl.ds(pl.multiple_of(rs * GRID_W, GRID_W), band)
            d0 = rs - r + NA_KR - 1
            s = lax.dot_general(q_ref[qs, hs], k_ref[ks, hs], (((1,), (1,)), ((), ())),
                                preferred_element_type=F32)
            group.append((qs, ks, d0, s))
        probs = []
        for qs, ks, d0, s in group:
            bias = jnp.concatenate([bias_ref[hh, d0 + 2 * p] for p in range(NA_KR // 2)],
                                   axis=1)
            s = s * scale + bias
            p = jnp.exp2(s - jnp.max(s, axis=-1, keepdims=True))
            l = jnp.sum(p, axis=-1, keepdims=True)
            probs.append((qs, ks, p.astype(BF16), l))
        for qs, ks, p, l in probs:
            o = jnp.dot(p, v_ref[ks, hs], preferred_element_type=F32)
            o_ref[qs, hs] = (o / l).astype(o_ref.dtype)

    n_groups = rows // NA_ROW_GROUP
    for hh in range(NA_HEADS_PER_STEP):
        if n_groups == 1:
            head_body(hh, 0)
        else:
            def body(it, carry, hh=hh):
                head_body(hh, it)
                return carry

            lax.fori_loop(0, n_groups, body, 0)


def _na_core(qkv, bias, layer, batch):
    m = qkv.shape[0]
    t = m // batch
    hps = NA_HEADS_PER_STEP
    assert NA_HEADS % hps == 0
    nb, w = NA_HEADS // hps, hps * NA_HD
    return pl.pallas_call(
        _na_kernel,
        grid=(batch, nb),
        in_specs=[
            pl.BlockSpec((t, w), lambda b, i: (b, i)),
            pl.BlockSpec((t, w), lambda b, i: (b, nb + i)),
            pl.BlockSpec((t, w), lambda b, i: (b, 2 * nb + i)),
            pl.BlockSpec((None, hps, NA_DR, GRID_W, NA_PAIR_W),
                         lambda b, i: (layer, i, 0, 0, 0)),
        ],
        out_specs=pl.BlockSpec((t, w), lambda b, i: (b, i)),
        out_shape=jax.ShapeDtypeStruct((m, D_MODEL), BF16),
        compiler_params=_params("parallel", "arbitrary"),
        name="na_core",
    )(qkv, qkv, qkv, bias)


def kernel(x, gla_w_in, gla_w_gate_up, gla_b_gate, gla_norm_g, gla_w_out, na_w_in, na_rpb, na_w_out, ffn_w_up, ffn_conv_w, ffn_conv_b, ffn_w_down, ln_mix_g, ln_mix_b, ln_ffn_g, ln_ffn_b):
    batch, seq, d = x.shape
    assert seq == SEQ and d == D_MODEL
    n_main = 2 * GLA_DK + 2 * GLA_DV
    gla_w = gla_w_in.astype(BF16)
    gla_w_lr = gla_w_in[:, :, n_main:].astype(BF16)
    gla_bg = gla_b_gate.reshape(gla_b_gate.shape[0], 2, 1, GLA_DK)
    gla_ng = gla_norm_g.reshape(gla_norm_g.shape[0], 1, GLA_DV_HEAD)
    na_w = na_w_in.astype(BF16)
    na_bias = _na_bias_tiles(na_rpb)
    w_up = ffn_w_up.astype(BF16)
    conv_b = ffn_conv_b.reshape(DEPTH, 1, 2 * D_FF)
    mix_g, mix_b = ln_mix_g.reshape(DEPTH, 1, d), ln_mix_b.reshape(DEPTH, 1, d)
    ffn_g, ffn_b = ln_ffn_g.reshape(DEPTH, 1, d), ln_ffn_b.reshape(DEPTH, 1, d)

    xf = x.reshape(batch * seq, d)
    for i in range(DEPTH):
        j = i // 2
        if i % 2 == 0:
            p, lr = _matmul(xf, gla_w, j, n_main, BF16, tm=PROJ_TM, tn=PROJ_TN, w_side=gla_w_lr)
            mix = _gla_core(p, lr, gla_w_gate_up, gla_bg, gla_ng, j, batch)
            w_out = gla_w_out
        else:
            qkv = _matmul(xf, na_w, j, 3 * d, BF16, tm=PROJ_TM, tn=PROJ_TN)
            mix = _na_core(qkv, na_bias, j, batch)
            w_out = na_w_out
        xf = _matmul_res_ln(mix, w_out, j, xf, mix_g, mix_b, i, tm=OUT_TM)
        xf = _conv_ffn_ln(xf, w_up, ffn_conv_w, conv_b, ffn_w_down, ffn_g, ffn_b, i,
                          tm=FFN_TM, tf=FFN_TF)
    return xf.reshape(batch, seq, d)
```

```python
import functools
import math

import jax
import jax.numpy as jnp
import numpy as np
from jax import lax
from jax.experimental import pallas as pl
from jax.experimental.pallas import tpu as pltpu

F32 = jnp.float32
BF16 = jnp.bfloat16

D_MODEL = 2048
SEQ = 2048
DEPTH = 4
GRID_W = 64
GLA_HEADS = 4
GLA_DK = D_MODEL // 2
GLA_DV = D_MODEL
GLA_DK_HEAD = GLA_DK // GLA_HEADS
GLA_DV_HEAD = GLA_DV // GLA_HEADS
GLA_RANK = 16
GLA_TEMP = 16.0
GLA_CHUNK = 64
GLA_GROUP = 4
GLA_LOCAL_GROUPS = 4
GLA_STATE_GROUP = 4
NA_HEADS = 16
NA_HD = D_MODEL // NA_HEADS
NA_KR = 8
NA_KC = 16
NA_HEADS_PER_STEP = 1
NA_ROW_GROUP = 32
D_FF = 11 * D_MODEL // 4
LN_EPS = 1e-5
RMS_EPS = 1e-6
NEG_INF = -1e9
ALPHA = (2.0 * DEPTH) ** 0.25
LOG2E = math.log2(math.e)

V7X_VMEM_BYTES = 64 * 1024 * 1024
VMEM_LIMIT = V7X_VMEM_BYTES - 8 * 1024 * 1024
LN_ROW_SPLIT = 4
OUT_TM = 512
PROJ_TM = 1024
PROJ_TN = 2048
FFN_TM = 1024
FFN_TF = 512
FFN_SUB = 256
HALO = 8


def _params(*sem):
    return pltpu.CompilerParams(dimension_semantics=sem, vmem_limit_bytes=VMEM_LIMIT)


def _layer_norm(y, g, b):
    mu = jnp.mean(y, axis=-1, keepdims=True)
    yc = y - mu
    var = jnp.mean(yc * yc, axis=-1, keepdims=True)
    return yc * lax.rsqrt(var + LN_EPS) * g + b


def _mm_kernel(x_ref, w_ref, o_ref):
    x = x_ref[...].astype(BF16)
    o_ref[...] = jnp.dot(x, w_ref[...], preferred_element_type=F32).astype(o_ref.dtype)


def _mm_side_kernel(x_ref, w_ref, ws_ref, o_ref, os_ref):
    x = x_ref[...].astype(BF16)
    o_ref[...] = jnp.dot(x, w_ref[...], preferred_element_type=F32).astype(o_ref.dtype)

    @pl.when(pl.program_id(1) == 0)
    def _side():
        os_ref[...] = jnp.dot(x, ws_ref[...], preferred_element_type=F32)


def _matmul(x, w, layer, n, out_dtype, tm, tn, w_side=None):
    m, k = x.shape
    assert m % tm == 0 and n % tn == 0 and n <= w.shape[2]
    in_specs = [pl.BlockSpec((tm, k), lambda i, j: (i, 0)),
                pl.BlockSpec((None, k, tn), lambda i, j: (layer, 0, j))]
    out_specs = pl.BlockSpec((tm, tn), lambda i, j: (i, j))
    out_shape = jax.ShapeDtypeStruct((m, n), out_dtype)
    if w_side is None:
        body, operands = _mm_kernel, (x, w)
    else:
        s = w_side.shape[2]
        body, operands = _mm_side_kernel, (x, w, w_side)
        in_specs.append(pl.BlockSpec((None, k, s), lambda i, j: (layer, 0, 0)))
        out_specs = (out_specs, pl.BlockSpec((tm, s), lambda i, j: (i, 0)))
        out_shape = (out_shape, jax.ShapeDtypeStruct((m, s), F32))
    return pl.pallas_call(
        body,
        grid=(m // tm, n // tn),
        in_specs=in_specs,
        out_specs=out_specs,
        out_shape=out_shape,
        compiler_params=_params("parallel", "arbitrary"),
        name="in_proj",
    )(*operands)


def _mm_ln_kernel(a_ref, w_ref, x_ref, g_ref, b_ref, o_ref, wbf_ref):
    @pl.when(pl.program_id(0) == 0)
    def _cast_weights():
        wbf_ref[...] = w_ref[...].astype(BF16)

    sub = a_ref.shape[0] // LN_ROW_SPLIT
    for r in range(LN_ROW_SPLIT):
        rs = slice(r * sub, (r + 1) * sub)
        m = jnp.dot(a_ref[rs, :], wbf_ref[...], preferred_element_type=F32)
        y = ALPHA * x_ref[rs, :] + m
        o_ref[rs, :] = _layer_norm(y, g_ref[...], b_ref[...])


def _matmul_res_ln(a, w, w_layer, x, g, b, ln_layer, tm):
    m, k = a.shape
    d = w.shape[2]
    return pl.pallas_call(
        _mm_ln_kernel,
        grid=(m // tm,),
        in_specs=[pl.BlockSpec((tm, k), lambda i: (i, 0)),
                  pl.BlockSpec((None, k, d), lambda i: (w_layer, 0, 0),
                               pipeline_mode=pl.Buffered(1)),
                  pl.BlockSpec((tm, d), lambda i: (i, 0)),
                  pl.BlockSpec((None, 1, d), lambda i: (ln_layer, 0, 0)),
                  pl.BlockSpec((None, 1, d), lambda i: (ln_layer, 0, 0))],
        out_specs=pl.BlockSpec((tm, d), lambda i: (i, 0)),
        out_shape=jax.ShapeDtypeStruct((m, d), F32),
        scratch_shapes=[pltpu.VMEM((k, d), BF16)],
        compiler_params=_params("arbitrary"),
        name="out_proj_ln",
    )(a, w, x, g, b)


def _gelu(a):
    return 0.5 * a * (1.0 + lax.erf(a * (2.0 ** -0.5)))


def _ffn_kernel(xp_ref, x_hbm, xn_ref, wa_ref, wb_ref, cw_ref, cb_ref, wd_ref, g_ref, b_ref,
                o_ref, xe_ref, xbuf_ref, x_sem, *, tm, tiles_per_seq):
    i = pl.program_id(0)
    j = pl.program_id(1)
    ni = pl.num_programs(0)
    nj = pl.num_programs(1)
    rows = tm + 2 * HALO
    tf = wd_ref.shape[0]
    d_ff = cw_ref.shape[1] // 2

    def x_copy(tile):
        return pltpu.make_async_copy(x_hbm.at[pl.ds(tile * tm, tm), :], xbuf_ref, x_sem)

    @pl.when(j == 0)
    def _stage_rows():
        @pl.when(i == 0)
        def _first_tile():
            x_copy(0).start()

        x_copy(i).wait()
        pos = i % tiles_per_seq
        keep_prev = (pos != 0).astype(F32)
        keep_next = (pos != tiles_per_seq - 1).astype(F32)
        staged = jnp.concatenate([xp_ref[...] * keep_prev, xbuf_ref[...], xn_ref[...] * keep_next],
                                 axis=0)
        xe_ref[...] = staged.astype(BF16)
        o_ref[...] = ALPHA * xbuf_ref[...]

    @pl.when((j == 1) & (i + 1 < ni))
    def _prefetch_next_tile():
        x_copy(i + 1).start()

    xe = xe_ref[...]

    def up(w_ref, col):
        return jnp.dot(xe, w_ref[:, col:col + FFN_SUB], preferred_element_type=F32)

    def conv(h, p_col):
        h_prev = pltpu.roll(h, 1, 0)[HALO:HALO + tm]
        h_next = pltpu.roll(h, rows - 1, 0)[HALO:HALO + tm]
        ps = pl.ds(pl.multiple_of(p_col, FFN_SUB), FFN_SUB)
        cw = cw_ref[:, ps]
        return (h_prev * cw[0:1] + h[HALO:HALO + tm] * cw[1:2] + h_next * cw[2:3]
                + cb_ref[:, ps])

    contrib = None
    for c in range(tf // FFN_SUB):
        col = c * FFN_SUB
        a = conv(up(wa_ref, col), j * tf + col)
        bb = conv(up(wb_ref, col), d_ff + j * tf + col)
        u = (_gelu(a) * bb).astype(BF16)
        wd = wd_ref[col:col + FFN_SUB, :].astype(BF16)
        part = jnp.dot(u, wd, preferred_element_type=F32)
        contrib = part if contrib is None else contrib + part
    o_ref[...] += contrib

    @pl.when(j == nj - 1)
    def _finish():
        o_ref[...] = _layer_norm(o_ref[...], g_ref[...], b_ref[...])


def _conv_ffn_ln(x, w_up, conv_w, conv_b, w_down, g, b, layer, tm, tf):
    m, d = x.shape
    f = w_down.shape[1]
    nf = f // tf
    assert m % tm == 0 and f % tf == 0 and SEQ % tm == 0 and tm % 8 == 0
    assert tf % FFN_SUB == 0 and nf >= 2
    tb = tm // 8
    last_blk = m // 8 - 1
    kern = functools.partial(_ffn_kernel, tm=tm, tiles_per_seq=SEQ // tm)
    return pl.pallas_call(
        kern,
        grid=(m // tm, nf),
        in_specs=[
            pl.BlockSpec((8, d), lambda i, j: (jnp.maximum(i * tb - 1, 0), 0)),
            pl.BlockSpec(memory_space=pl.ANY),
            pl.BlockSpec((8, d), lambda i, j: (jnp.minimum((i + 1) * tb, last_blk), 0)),
            pl.BlockSpec((None, d, tf), lambda i, j: (layer, 0, j)),
            pl.BlockSpec((None, d, tf), lambda i, j: (layer, 0, j + nf)),
            pl.BlockSpec((None, 3, 2 * f), lambda i, j: (layer, 0, 0)),
            pl.BlockSpec((None, 1, 2 * f), lambda i, j: (layer, 0, 0)),
            pl.BlockSpec((None, tf, d), lambda i, j: (layer, j, 0)),
            pl.BlockSpec((None, 1, d), lambda i, j: (layer, 0, 0)),
            pl.BlockSpec((None, 1, d), lambda i, j: (layer, 0, 0)),
        ],
        out_specs=pl.BlockSpec((tm, d), lambda i, j: (i, 0), pipeline_mode=pl.Buffered(1)),
        out_shape=jax.ShapeDtypeStruct((m, d), F32),
        scratch_shapes=[pltpu.VMEM((tm + 2 * HALO, d), BF16),
                        pltpu.VMEM((tm, d), F32),
                        pltpu.SemaphoreType.DMA(())],
        compiler_params=_params("arbitrary", "arbitrary"),
        name="conv_ffn_ln",
    )(x, x, x, w_up, w_up, conv_w, conv_b, w_down, g, b)


def _bf16_terms(x, n_terms):
    terms = []
    for _ in range(n_terms - 1):
        t = x.astype(BF16)
        terms.append(t)
        x = x - t.astype(F32)
    terms.append(x.astype(BF16))
    return terms


def _split_dot(tri, x):
    t = tri.astype(BF16)
    hi, lo = _bf16_terms(x, 2)
    return (jnp.dot(t, hi, preferred_element_type=F32)
            + jnp.dot(t, lo, preferred_element_type=F32))


def _log_sigmoid(x):
    return jnp.minimum(x, 0.0) - jnp.log(1.0 + jnp.exp(-jnp.abs(x)))


def _gla_kernel(q_ref, k_ref, v_ref, g_ref, lr_ref, wg_ref, bg_ref, ng_ref, o_ref,
                la_ref, qg_ref, kg_ref, dg_ref, of_ref, ob_ref, sf_ref, sb_ref):
    t = q_ref.shape[0]
    c = GLA_CHUNK
    gp = GLA_GROUP
    gr = gp * c
    n_groups = t // gr
    scale = GLA_DK_HEAD ** -0.5
    nt = (((1,), (1,)), ((), ()))
    tn = (((0,), (0,)), ((), ()))

    lr = lr_ref[...]
    for z in range(2):
        logit = jnp.dot(lr[:, z * GLA_RANK:(z + 1) * GLA_RANK].astype(BF16),
                        wg_ref[z].astype(BF16), preferred_element_type=F32)
        la_ref[z] = _log_sigmoid(logit + bg_ref[z]) * (LOG2E / GLA_TEMP)

    row = lax.broadcasted_iota(jnp.int32, (c, c), 0)
    col = lax.broadcasted_iota(jnp.int32, (c, c), 1)
    tri = ((row >= col).astype(F32), (row <= col).astype(F32))
    tot_row = (c - 1, 0)
    qrow = lax.broadcasted_iota(jnp.int32, (c, gr), 0)
    klane = lax.broadcasted_iota(jnp.int32, (c, gr), 1)
    s_refs = (sf_ref, sb_ref)
    o_refs = (of_ref, ob_ref)

    def sum_tots(tots, lo, hi):
        acc = jnp.zeros_like(tots[0])
        for p in range(lo, hi):
            acc = acc + tots[p]
        return acc

    def chunk_slices(g):
        return [pl.ds(pl.multiple_of(g * gr + p * c, c), c) for p in range(gp)]

    def local_cums(z, g):
        return [_split_dot(tri[z], la_ref[z, sl, :]) for sl in chunk_slices(g)]

    def local_scores(z, g, cums):
        sls = chunk_slices(g)
        tots = [cum[tot_row[z]:tot_row[z] + 1, :] for cum in cums]
        qd, ki, ks = [], [], []
        for sl, cum, tot in zip(sls, cums, tots):
            q = q_ref[sl, :].astype(F32) * scale
            k = k_ref[sl, :].astype(F32)
            qd.append(q * jnp.exp2(cum))
            ki.append((k * jnp.exp2(-cum)).astype(BF16))
            ks.append(k * jnp.exp2(tot - cum))
        if z == 0:
            before = [sum_tots(tots, 0, p) for p in range(gp)]
            after = [sum_tots(tots, p + 1, gp) for p in range(gp)]
        else:
            before = [sum_tots(tots, p + 1, gp) for p in range(gp)]
            after = [sum_tots(tots, 0, p) for p in range(gp)]
        for p in range(gp):
            qg_ref[z, sls[p], :] = (qd[p] * jnp.exp2(before[p])).astype(BF16)
            kg_ref[z, sls[p], :] = (ks[p] * jnp.exp2(after[p])).astype(BF16)
        dg_ref[z, pl.ds(g, 1), :] = jnp.exp2(sum_tots(tots, 0, gp))
        scores = []
        for pq in range(gp):
            keys = []
            for pk in range(gp):
                seen = pk < pq if z == 0 else pk > pq
                if pk == pq:
                    keys.append(ki[pk])
                elif not seen:
                    keys.append(jnp.zeros_like(ki[pk]))
                else:
                    lo, hi = (pk + 1, pq) if z == 0 else (pq + 1, pk)
                    keys.append((ks[pk] * jnp.exp2(sum_tots(tots, lo, hi))).astype(BF16))
            scores.append(lax.dot_general(qd[pq].astype(BF16), jnp.concatenate(keys, axis=0),
                                          nt, preferred_element_type=F32))
        return scores

    def local_out(z, g, scores):
        probs = []
        for pq, s in enumerate(scores):
            visible = (klane <= qrow + pq * c) if z == 0 else (klane > qrow + pq * c)
            probs.append(jnp.where(visible, s, 0.0).astype(BF16))
        rows = pl.ds(pl.multiple_of(g * gr, gr), gr)
        o_refs[z][rows, :] = jnp.dot(jnp.concatenate(probs, axis=0), v_ref[rows, :],
                                     preferred_element_type=F32)

    def local_body(it, carry):
        work = [(z, it * GLA_LOCAL_GROUPS + u) for u in range(GLA_LOCAL_GROUPS) for z in range(2)]
        cums = [local_cums(z, g) for z, g in work]
        scores = [local_scores(z, g, cum) for (z, g), cum in zip(work, cums)]
        for (z, g), s in zip(work, scores):
            local_out(z, g, s)
        return carry

    lax.fori_loop(0, n_groups // GLA_LOCAL_GROUPS, local_body, 0)

    sf_ref[...] = jnp.zeros_like(sf_ref)
    sb_ref[...] = jnp.zeros_like(sb_ref)

    def state_body(it, carry):
        steps = []
        for z in range(2):
            for u in range(GLA_STATE_GROUP):
                i = it * GLA_STATE_GROUP + u
                g = i if z == 0 else n_groups - 1 - i
                rows = pl.ds(pl.multiple_of(g * gr, gr), gr)
                upd = lax.dot_general(v_ref[rows, :], kg_ref[z, rows, :], tn,
                                      preferred_element_type=F32)
                steps.append((z, g, rows, upd))
        st = [sf_ref[...], sb_ref[...]]
        for z, g, rows, upd in steps:
            o_refs[z][rows, :] += lax.dot_general(qg_ref[z, rows, :], st[z].astype(BF16), nt,
                                                  preferred_element_type=F32)
            st[z] = st[z] * dg_ref[z, pl.ds(g, 1), :] + upd
        sf_ref[...] = st[0]
        sb_ref[...] = st[1]
        return carry

    lax.fori_loop(0, n_groups // GLA_STATE_GROUP, state_body, 0)

    o = of_ref[...] + ob_ref[...]
    o = o * lax.rsqrt(jnp.mean(o * o, axis=-1, keepdims=True) + RMS_EPS) * ng_ref[...]
    gate = g_ref[...].astype(F32)
    o_ref[...] = (o * (gate * jax.nn.sigmoid(gate))).astype(o_ref.dtype)


def _gla_core(p, lr, w_gate, b_gate, norm_g, layer, batch):
    m = p.shape[0]
    t = m // batch
    h, dk, dv = GLA_HEADS, GLA_DK_HEAD, GLA_DV_HEAD
    return pl.pallas_call(
        _gla_kernel,
        grid=(batch, h),
        in_specs=[
            pl.BlockSpec((t, dk), lambda b, i: (b, i)),
            pl.BlockSpec((t, dk), lambda b, i: (b, h + i)),
            pl.BlockSpec((t, dv), lambda b, i: (b, h + i)),
            pl.BlockSpec((t, dv), lambda b, i: (b, 2 * h + i)),
            pl.BlockSpec((t, 2 * GLA_RANK), lambda b, i: (b, 0)),
            pl.BlockSpec((None, 2, GLA_RANK, dk), lambda b, i: (layer, 0, 0, i)),
            pl.BlockSpec((None, 2, 1, dk), lambda b, i: (layer, 0, 0, i)),
            pl.BlockSpec((None, 1, dv), lambda b, i: (layer, 0, 0)),
        ],
        out_specs=pl.BlockSpec((t, dv), lambda b, i: (b, i)),
        out_shape=jax.ShapeDtypeStruct((m, GLA_DV), BF16),
        scratch_shapes=[pltpu.VMEM((2, t, dk), F32),
                        pltpu.VMEM((2, t, dk), BF16), pltpu.VMEM((2, t, dk), BF16),
                        pltpu.VMEM((2, t // (GLA_CHUNK * GLA_GROUP), dk), F32),
                        pltpu.VMEM((t, dv), F32), pltpu.VMEM((t, dv), F32),
                        pltpu.VMEM((dv, dk), F32), pltpu.VMEM((dv, dk), F32)],
        compiler_params=_params("parallel", "arbitrary"),
        name="gla_core",
    )(p, p, p, p, lr, w_gate, b_gate, norm_g)


NA_DR = 2 * NA_KR
NA_DC = 2 * NA_KC - 1
NA_PAIR_W = 2 * GRID_W


def _na_col_geometry(shape):
    log_w = GRID_W.bit_length() - 1
    assert 1 << log_w == GRID_W
    col = lax.broadcasted_iota(jnp.int32, shape, 1)
    qc = col >> (log_w + 1)
    half = (col >> log_w) & 1
    kc = col & (GRID_W - 1)
    win = jnp.clip(qc - NA_KC // 2, 0, GRID_W - NA_KC)
    valid = (kc >= win) & (kc < win + NA_KC)
    src = half * NA_DC + (kc - qc + NA_KC - 1)
    return src, valid


def _na_bias_kernel(rpb2_ref, o_ref, sel_ref, valid_ref):
    n = o_ref.shape[1]

    @pl.when((pl.program_id(0) == 0) & (pl.program_id(1) == 0))
    def _build_selector():
        src, valid = _na_col_geometry((GRID_W, n))
        r = lax.broadcasted_iota(jnp.int32, (GRID_W, n), 0)
        sel_ref[...] = jnp.where(valid & (r == src), 1.0, 0.0).astype(BF16)
        _, valid_o = _na_col_geometry((NA_DR, n))
        valid_ref[...] = jnp.where(valid_o, 1.0, 0.0)

    sel = sel_ref[...]
    acc = jnp.zeros(o_ref.shape, F32)
    for term in _bf16_terms(rpb2_ref[...], 3):
        acc = acc + jnp.dot(term, sel, preferred_element_type=F32)
    o_ref[...] = jnp.where(valid_ref[...] > 0.5, acc, NEG_INF) * LOG2E


def _na_bias_tiles(rpb):
    nl, nh = rpb.shape[:2]
    assert rpb.shape[2] == NA_DR - 1 and rpb.shape[3] == NA_DC and 2 * NA_DC <= GRID_W
    rp = jnp.pad(rpb.astype(F32), ((0, 0), (0, 0), (0, 2), (0, 0)))
    pad = jnp.zeros((nl, nh, NA_DR, GRID_W - 2 * NA_DC), F32)
    rpb2 = jnp.concatenate([rp[:, :, :NA_DR], rp[:, :, 1:NA_DR + 1], pad], axis=-1)
    n = GRID_W * NA_PAIR_W
    tiles = pl.pallas_call(
        _na_bias_kernel,
        grid=(nl, nh),
        in_specs=[pl.BlockSpec((None, None, NA_DR, GRID_W), lambda l, h: (l, h, 0, 0))],
        out_specs=pl.BlockSpec((None, None, NA_DR, n), lambda l, h: (l, h, 0, 0)),
        out_shape=jax.ShapeDtypeStruct((nl, nh, NA_DR, n), F32),
        scratch_shapes=[pltpu.VMEM((GRID_W, n), BF16), pltpu.VMEM((NA_DR, n), F32)],
        compiler_params=_params("arbitrary", "arbitrary"),
        name="na_bias",
    )(rpb2)
    return tiles.reshape(nl, nh, NA_DR, GRID_W, NA_PAIR_W)


def _na_kernel(q_ref, k_ref, v_ref, bias_ref, o_ref):
    rows = q_ref.shape[0] // GRID_W
    band = NA_KR * GRID_W
    scale = NA_HD ** -0.5 * LOG2E

    def head_body(hh, it):
        hs = slice(hh * NA_HD, (hh + 1) * NA_HD)
        group = []
        for u in range(NA_ROW_GROUP):
            r = it * NA_ROW_GROUP + u
            if isinstance(r, int):
                rs = min(max(r - NA_KR // 2, 0), rows - NA_KR)
                qs, ks = pl.ds(r * GRID_W, GRID_W), pl.ds(rs * GRID_W, band)
            else:
                rs = jnp.clip(r - NA_KR // 2, 0, rows - NA_KR)
                qs = pl.ds(pl.multiple_of(r * GRID_W, GRID_W), GRID_W)
                ks = pl.ds(pl.multiple_of(rs * GRID_W, GRID_W), band)
            d0 = rs - r + NA_KR - 1
            s = lax.dot_general(q_ref[qs, hs], k_ref[ks, hs], (((1,), (1,)), ((), ())),
                                preferred_element_type=F32)
            group.append((qs, ks, d0, s))
        probs = []
        for qs, ks, d0, s in group:
            bias = jnp.concatenate([bias_ref[hh, d0 + 2 * p] for p in range(NA_KR // 2)],
                                   axis=1)
            s = s * scale + bias
            p = jnp.exp2(s - jnp.max(s, axis=-1, keepdims=True))
            l = jnp.sum(p, axis=-1, keepdims=True)
            probs.append((qs, ks, p.astype(BF16), l))
        for qs, ks, p, l in probs:
            o = jnp.dot(p, v_ref[ks, hs], preferred_element_type=F32)
            o_ref[qs, hs] = (o / l).astype(o_ref.dtype)

    n_groups = rows // NA_ROW_GROUP
    for hh in range(NA_HEADS_PER_STEP):
        if n_groups == 1:
            head_body(hh, 0)
        else:
            def body(it, carry, hh=hh):
                head_body(hh, it)
                return carry

            lax.fori_loop(0, n_groups, body, 0)


def _na_core(qkv, bias, layer, batch):
    m = qkv.shape[0]
    t = m // batch
    hps = NA_HEADS_PER_STEP
    assert NA_HEADS % hps == 0
    nb, w = NA_HEADS // hps, hps * NA_HD
    return pl.pallas_call(
        _na_kernel,
        grid=(batch, nb),
        in_specs=[
            pl.BlockSpec((t, w), lambda b, i: (b, i)),
            pl.BlockSpec((t, w), lambda b, i: (b, nb + i)),
            pl.BlockSpec((t, w), lambda b, i: (b, 2 * nb + i)),
            pl.BlockSpec((None, hps, NA_DR, GRID_W, NA_PAIR_W),
                         lambda b, i: (layer, i, 0, 0, 0)),
        ],
        out_specs=pl.BlockSpec((t, w), lambda b, i: (b, i)),
        out_shape=jax.ShapeDtypeStruct((m, D_MODEL), BF16),
        compiler_params=_params("parallel", "arbitrary"),
        name="na_core",
    )(qkv, qkv, qkv, bias)


def kernel(x, gla_w_in, gla_w_gate_up, gla_b_gate, gla_norm_g, gla_w_out, na_w_in, na_rpb, na_w_out, ffn_w_up, ffn_conv_w, ffn_conv_b, ffn_w_down, ln_mix_g, ln_mix_b, ln_ffn_g, ln_ffn_b):
    batch, seq, d = x.shape
    assert seq == SEQ and d == D_MODEL
    n_main = 2 * GLA_DK + 2 * GLA_DV
    gla_w = gla_w_in.astype(BF16)
    gla_w_lr = gla_w_in[:, :, n_main:].astype(BF16)
    gla_bg = gla_b_gate.reshape(gla_b_gate.shape[0], 2, 1, GLA_DK)
    gla_ng = gla_norm_g.reshape(gla_norm_g.shape[0], 1, GLA_DV_HEAD)
    na_w = na_w_in.astype(BF16)
    na_bias = _na_bias_tiles(na_rpb)
    w_up = ffn_w_up.astype(BF16)
    conv_b = ffn_conv_b.reshape(DEPTH, 1, 2 * D_FF)
    mix_g, mix_b = ln_mix_g.reshape(DEPTH, 1, d), ln_mix_b.reshape(DEPTH, 1, d)
    ffn_g, ffn_b = ln_ffn_g.reshape(DEPTH, 1, d), ln_ffn_b.reshape(DEPTH, 1, d)

    xf = x.reshape(batch * seq, d)
    for i in range(DEPTH):
        j = i // 2
        if i % 2 == 0:
            p, lr = _matmul(xf, gla_w, j, n_main, BF16, tm=PROJ_TM, tn=PROJ_TN, w_side=gla_w_lr)
            mix = _gla_core(p, lr, gla_w_gate_up, gla_bg, gla_ng, j, batch)
            w_out = gla_w_out
        else:
            qkv = _matmul(xf, na_w, j, 3 * d, BF16, tm=PROJ_TM, tn=PROJ_TN)
            mix = _na_core(qkv, na_bias, j, batch)
            w_out = na_w_out
        xf = _matmul_res_ln(mix, w_out, j, xf, mix_g, mix_b, i, tm=OUT_TM)
        xf = _conv_ffn_ln(xf, w_up, ffn_conv_w, conv_b, ffn_w_down, ffn_g, ffn_b, i,
                          tm=FFN_TM, tf=FFN_TF)
    return xf.reshape(batch, seq, d)
```

```python
import functools
import math

import jax
import jax.numpy as jnp
import numpy as np
from jax import lax
from jax.experimental import pallas as pl
from jax.experimental.pallas import tpu as pltpu

F32 = jnp.float32
BF16 = jnp.bfloat16

D_MODEL = 2048
SEQ = 2048
DEPTH = 4
GRID_W = 64
GLA_HEADS = 4
GLA_DK = D_MODEL // 2
GLA_DV = D_MODEL
GLA_DK_HEAD = GLA_DK // GLA_HEADS
GLA_DV_HEAD = GLA_DV // GLA_HEADS
GLA_RANK = 16
GLA_TEMP = 16.0
GLA_CHUNK = 64
GLA_GROUP = 4
GLA_LOCAL_GROUPS = 4
GLA_STATE_GROUP = 4
NA_HEADS = 16
NA_HD = D_MODEL // NA_HEADS
NA_KR = 8
NA_KC = 16
NA_HEADS_PER_STEP = 1
NA_ROW_GROUP = 32
D_FF = 11 * D_MODEL // 4
LN_EPS = 1e-5
RMS_EPS = 1e-6
NEG_INF = -1e9
ALPHA = (2.0 * DEPTH) ** 0.25
LOG2E = math.log2(math.e)

V7X_VMEM_BYTES = 64 * 1024 * 1024
VMEM_LIMIT = V7X_VMEM_BYTES - 8 * 1024 * 1024
LN_ROW_SPLIT = 4
OUT_TM = 512
PROJ_TM = 1024
PROJ_TN = 2048
FFN_TM = 1024
FFN_TF = 512
FFN_SUB = 512
HALO = 8


def _params(*sem):
    return pltpu.CompilerParams(dimension_semantics=sem, vmem_limit_bytes=VMEM_LIMIT)


def _layer_norm(y, g, b):
    mu = jnp.mean(y, axis=-1, keepdims=True)
    yc = y - mu
    var = jnp.mean(yc * yc, axis=-1, keepdims=True)
    return yc * lax.rsqrt(var + LN_EPS) * g + b


def _mm_kernel(x_ref, w_ref, o_ref):
    x = x_ref[...].astype(BF16)
    o_ref[...] = jnp.dot(x, w_ref[...], preferred_element_type=F32).astype(o_ref.dtype)


def _mm_side_kernel(x_ref, w_ref, ws_ref, o_ref, os_ref):
    x = x_ref[...].astype(BF16)
    o_ref[...] = jnp.dot(x, w_ref[...], preferred_element_type=F32).astype(o_ref.dtype)

    @pl.when(pl.program_id(1) == 0)
    def _side():
        os_ref[...] = jnp.dot(x, ws_ref[...], preferred_element_type=F32)


def _matmul(x, w, layer, n, out_dtype, tm, tn, w_side=None):
    m, k = x.shape
    assert m % tm == 0 and n % tn == 0 and n <= w.shape[2]
    in_specs = [pl.BlockSpec((tm, k), lambda i, j: (i, 0)),
                pl.BlockSpec((None, k, tn), lambda i, j: (layer, 0, j))]
    out_specs = pl.BlockSpec((tm, tn), lambda i, j: (i, j))
    out_shape = jax.ShapeDtypeStruct((m, n), out_dtype)
    if w_side is None:
        body, operands = _mm_kernel, (x, w)
    else:
        s = w_side.shape[2]
        body, operands = _mm_side_kernel, (x, w, w_side)
        in_specs.append(pl.BlockSpec((None, k, s), lambda i, j: (layer, 0, 0)))
        out_specs = (out_specs, pl.BlockSpec((tm, s), lambda i, j: (i, 0)))
        out_shape = (out_shape, jax.ShapeDtypeStruct((m, s), F32))
    return pl.pallas_call(
        body,
        grid=(m // tm, n // tn),
        in_specs=in_specs,
        out_specs=out_specs,
        out_shape=out_shape,
        compiler_params=_params("parallel", "arbitrary"),
        name="in_proj",
    )(*operands)


def _mm_ln_kernel(a_ref, w_ref, x_ref, g_ref, b_ref, o_ref, wbf_ref):
    @pl.when(pl.program_id(0) == 0)
    def _cast_weights():
        wbf_ref[...] = w_ref[...].astype(BF16)

    sub = a_ref.shape[0] // LN_ROW_SPLIT
    for r in range(LN_ROW_SPLIT):
        rs = slice(r * sub, (r + 1) * sub)
        m = jnp.dot(a_ref[rs, :], wbf_ref[...], preferred_element_type=F32)
        y = ALPHA * x_ref[rs, :] + m
        o_ref[rs, :] = _layer_norm(y, g_ref[...], b_ref[...])


def _matmul_res_ln(a, w, w_layer, x, g, b, ln_layer, tm):
    m, k = a.shape
    d = w.shape[2]
    return pl.pallas_call(
        _mm_ln_kernel,
        grid=(m // tm,),
        in_specs=[pl.BlockSpec((tm, k), lambda i: (i, 0)),
                  pl.BlockSpec((None, k, d), lambda i: (w_layer, 0, 0),
                               pipeline_mode=pl.Buffered(1)),
                  pl.BlockSpec((tm, d), lambda i: (i, 0)),
                  pl.BlockSpec((None, 1, d), lambda i: (ln_layer, 0, 0)),
                  pl.BlockSpec((None, 1, d), lambda i: (ln_layer, 0, 0))],
        out_specs=pl.BlockSpec((tm, d), lambda i: (i, 0)),
        out_shape=jax.ShapeDtypeStruct((m, d), F32),
        scratch_shapes=[pltpu.VMEM((k, d), BF16)],
        compiler_params=_params("arbitrary"),
        name="out_proj_ln",
    )(a, w, x, g, b)


def _gelu(a):
    return 0.5 * a * (1.0 + lax.erf(a * (2.0 ** -0.5)))


def _ffn_kernel(xp_ref, x_hbm, xn_ref, wa_ref, wb_ref, cw_ref, cb_ref, wd_ref, g_ref, b_ref,
                o_ref, xe_ref, xbuf_ref, x_sem, *, tm, tiles_per_seq):
    i = pl.program_id(0)
    j = pl.program_id(1)
    ni = pl.num_programs(0)
    nj = pl.num_programs(1)
    rows = tm + 2 * HALO
    tf = wd_ref.shape[0]
    d_ff = cw_ref.shape[1] // 2

    def x_copy(tile):
        return pltpu.make_async_copy(x_hbm.at[pl.ds(tile * tm, tm), :], xbuf_ref, x_sem)

    @pl.when(j == 0)
    def _stage_rows():
        @pl.when(i == 0)
        def _first_tile():
            x_copy(0).start()

        x_copy(i).wait()
        pos = i % tiles_per_seq
        keep_prev = (pos != 0).astype(F32)
        keep_next = (pos != tiles_per_seq - 1).astype(F32)
        staged = jnp.concatenate([xp_ref[...] * keep_prev, xbuf_ref[...], xn_ref[...] * keep_next],
                                 axis=0)
        xe_ref[...] = staged.astype(BF16)
        o_ref[...] = ALPHA * xbuf_ref[...]

    @pl.when((j == 1) & (i + 1 < ni))
    def _prefetch_next_tile():
        x_copy(i + 1).start()

    xe = xe_ref[...]

    def up(w_ref, col):
        return jnp.dot(xe, w_ref[:, col:col + FFN_SUB], preferred_element_type=F32)

    def conv(h, p_col):
        h_prev = pltpu.roll(h, 1, 0)[HALO:HALO + tm]
        h_next = pltpu.roll(h, rows - 1, 0)[HALO:HALO + tm]
        ps = pl.ds(pl.multiple_of(p_col, FFN_SUB), FFN_SUB)
        cw = cw_ref[:, ps]
        return (h_prev * cw[0:1] + h[HALO:HALO + tm] * cw[1:2] + h_next * cw[2:3]
                + cb_ref[:, ps])

    contrib = None
    for c in range(tf // FFN_SUB):
        col = c * FFN_SUB
        a = conv(up(wa_ref, col), j * tf + col)
        bb = conv(up(wb_ref, col), d_ff + j * tf + col)
        u = (_gelu(a) * bb).astype(BF16)
        wd = wd_ref[col:col + FFN_SUB, :].astype(BF16)
        part = jnp.dot(u, wd, preferred_element_type=F32)
        contrib = part if contrib is None else contrib + part
    o_ref[...] += contrib

    @pl.when(j == nj - 1)
    def _finish():
        o_ref[...] = _layer_norm(o_ref[...], g_ref[...], b_ref[...])


def _conv_ffn_ln(x, w_up, conv_w, conv_b, w_down, g, b, layer, tm, tf):
    m, d = x.shape
    f = w_down.shape[1]
    nf = f // tf
    assert m % tm == 0 and f % tf == 0 and SEQ % tm == 0 and tm % 8 == 0
    assert tf % FFN_SUB == 0 and nf >= 2
    tb = tm // 8
    last_blk = m // 8 - 1
    kern = functools.partial(_ffn_kernel, tm=tm, tiles_per_seq=SEQ // tm)
    return pl.pallas_call(
        kern,
        grid=(m // tm, nf),
        in_specs=[
            pl.BlockSpec((8, d), lambda i, j: (jnp.maximum(i * tb - 1, 0), 0)),
            pl.BlockSpec(memory_space=pl.ANY),
            pl.BlockSpec((8, d), lambda i, j: (jnp.minimum((i + 1) * tb, last_blk), 0)),
            pl.BlockSpec((None, d, tf), lambda i, j: (layer, 0, j)),
            pl.BlockSpec((None, d, tf), lambda i, j: (layer, 0, j + nf)),
            pl.BlockSpec((None, 3, 2 * f), lambda i, j: (layer, 0, 0)),
            pl.BlockSpec((None, 1, 2 * f), lambda i, j: (layer, 0, 0)),
            pl.BlockSpec((None, tf, d), lambda i, j: (layer, j, 0)),
            pl.BlockSpec((None, 1, d), lambda i, j: (layer, 0, 0)),
            pl.BlockSpec((None, 1, d), lambda i, j: (layer, 0, 0)),
        ],
        out_specs=pl.BlockSpec((tm, d), lambda i, j: (i, 0), pipeline_mode=pl.Buffered(1)),
        out_shape=jax.ShapeDtypeStruct((m, d), F32),
        scratch_shapes=[pltpu.VMEM((tm + 2 * HALO, d), BF16),
                        pltpu.VMEM((tm, d), F32),
                        pltpu.SemaphoreType.DMA(())],
        compiler_params=_params("arbitrary", "arbitrary"),
        name="conv_ffn_ln",
    )(x, x, x, w_up, w_up, conv_w, conv_b, w_down, g, b)


def _bf16_terms(x, n_terms):
    terms = []
    for _ in range(n_terms - 1):
        t = x.astype(BF16)
        terms.append(t)
        x = x - t.astype(F32)
    terms.append(x.astype(BF16))
    return terms


def _split_dot(tri, x):
    t = tri.astype(BF16)
    hi, lo = _bf16_terms(x, 2)
    return (jnp.dot(t, hi, preferred_element_type=F32)
            + jnp.dot(t, lo, preferred_element_type=F32))


def _log_sigmoid(x):
    return jnp.minimum(x, 0.0) - jnp.log(1.0 + jnp.exp(-jnp.abs(x)))


def _gla_kernel(q_ref, k_ref, v_ref, g_ref, lr_ref, wg_ref, bg_ref, ng_ref, o_ref,
                la_ref, qg_ref, kg_ref, dg_ref, of_ref, ob_ref, sf_ref, sb_ref):
    t = q_ref.shape[0]
    c = GLA_CHUNK
    gp = GLA_GROUP
    gr = gp * c
    n_groups = t // gr
    scale = GLA_DK_HEAD ** -0.5
    nt = (((1,), (1,)), ((), ()))
    tn = (((0,), (0,)), ((), ()))

    lr = lr_ref[...]
    for z in range(2):
        logit = jnp.dot(lr[:, z * GLA_RANK:(z + 1) * GLA_RANK].astype(BF16),
                        wg_ref[z].astype(BF16), preferred_element_type=F32)
        la_ref[z] = _log_sigmoid(logit + bg_ref[z]) * (LOG2E / GLA_TEMP)

    row = lax.broadcasted_iota(jnp.int32, (c, c), 0)
    col = lax.broadcasted_iota(jnp.int32, (c, c), 1)
    tri = ((row >= col).astype(F32), (row <= col).astype(F32))
    tot_row = (c - 1, 0)
    qrow = lax.broadcasted_iota(jnp.int32, (c, gr), 0)
    klane = lax.broadcasted_iota(jnp.int32, (c, gr), 1)
    s_refs = (sf_ref, sb_ref)
    o_refs = (of_ref, ob_ref)

    def sum_tots(tots, lo, hi):
        acc = jnp.zeros_like(tots[0])
        for p in range(lo, hi):
            acc = acc + tots[p]
        return acc

    def chunk_slices(g):
        return [pl.ds(pl.multiple_of(g * gr + p * c, c), c) for p in range(gp)]

    def local_cums(z, g):
        return [_split_dot(tri[z], la_ref[z, sl, :]) for sl in chunk_slices(g)]

    def local_scores(z, g, cums):
        sls = chunk_slices(g)
        tots = [cum[tot_row[z]:tot_row[z] + 1, :] for cum in cums]
        qd, ki, ks = [], [], []
        for sl, cum, tot in zip(sls, cums, tots):
            q = q_ref[sl, :].astype(F32) * scale
            k = k_ref[sl, :].astype(F32)
            qd.append(q * jnp.exp2(cum))
            ki.append((k * jnp.exp2(-cum)).astype(BF16))
            ks.append(k * jnp.exp2(tot - cum))
        if z == 0:
            before = [sum_tots(tots, 0, p) for p in range(gp)]
            after = [sum_tots(tots, p + 1, gp) for p in range(gp)]
        else:
            before = [sum_tots(tots, p + 1, gp) for p in range(gp)]
            after = [sum_tots(tots, 0, p) for p in range(gp)]
        for p in range(gp):
            qg_ref[z, sls[p], :] = (qd[p] * jnp.exp2(before[p])).astype(BF16)
            kg_ref[z, sls[p], :] = (ks[p] * jnp.exp2(after[p])).astype(BF16)
        dg_ref[z, pl.ds(g, 1), :] = jnp.exp2(sum_tots(tots, 0, gp))
        scores = []
        for pq in range(gp):
            keys = []
            for pk in range(gp):
                seen = pk < pq if z == 0 else pk > pq
                if pk == pq:
                    keys.append(ki[pk])
                elif not seen:
                    keys.append(jnp.zeros_like(ki[pk]))
                else:
                    lo, hi = (pk + 1, pq) if z == 0 else (pq + 1, pk)
                    keys.append((ks[pk] * jnp.exp2(sum_tots(tots, lo, hi))).astype(BF16))
            scores.append(lax.dot_general(qd[pq].astype(BF16), jnp.concatenate(keys, axis=0),
                                          nt, preferred_element_type=F32))
        return scores

    def local_out(z, g, scores):
        probs = []
        for pq, s in enumerate(scores):
            visible = (klane <= qrow + pq * c) if z == 0 else (klane > qrow + pq * c)
            probs.append(jnp.where(visible, s, 0.0).astype(BF16))
        rows = pl.ds(pl.multiple_of(g * gr, gr), gr)
        o_refs[z][rows, :] = jnp.dot(jnp.concatenate(probs, axis=0), v_ref[rows, :],
                                     preferred_element_type=F32)

    def local_body(it, carry):
        work = [(z, it * GLA_LOCAL_GROUPS + u) for u in range(GLA_LOCAL_GROUPS) for z in range(2)]
        cums = [local_cums(z, g) for z, g in work]
        scores = [local_scores(z, g, cum) for (z, g), cum in zip(work, cums)]
        for (z, g), s in zip(work, scores):
            local_out(z, g, s)
        return carry

    lax.fori_loop(0, n_groups // GLA_LOCAL_GROUPS, local_body, 0)

    sf_ref[...] = jnp.zeros_like(sf_ref)
    sb_ref[...] = jnp.zeros_like(sb_ref)

    def state_body(it, carry):
        steps = []
        for z in range(2):
            for u in range(GLA_STATE_GROUP):
                i = it * GLA_STATE_GROUP + u
                g = i if z == 0 else n_groups - 1 - i
                rows = pl.ds(pl.multiple_of(g * gr, gr), gr)
                upd = lax.dot_general(v_ref[rows, :], kg_ref[z, rows, :], tn,
                                      preferred_element_type=F32)
                steps.append((z, g, rows, upd))
        st = [sf_ref[...], sb_ref[...]]
        for z, g, rows, upd in steps:
            o_refs[z][rows, :] += lax.dot_general(qg_ref[z, rows, :], st[z].astype(BF16), nt,
                                                  preferred_element_type=F32)
            st[z] = st[z] * dg_ref[z, pl.ds(g, 1), :] + upd
        sf_ref[...] = st[0]
        sb_ref[...] = st[1]
        return carry

    lax.fori_loop(0, n_groups // GLA_STATE_GROUP, state_body, 0)

    o = of_ref[...] + ob_ref[...]
    o = o * lax.rsqrt(jnp.mean(o * o, axis=-1, keepdims=True) + RMS_EPS) * ng_ref[...]
    gate = g_ref[...].astype(F32)
    o_ref[...] = (o * (gate * jax.nn.sigmoid(gate))).astype(o_ref.dtype)


def _gla_core(p, lr, w_gate, b_gate, norm_g, layer, batch):
    m = p.shape[0]
    t = m // batch
    h, dk, dv = GLA_HEADS, GLA_DK_HEAD, GLA_DV_HEAD
    return pl.pallas_call(
        _gla_kernel,
        grid=(batch, h),
        in_specs=[
            pl.BlockSpec((t, dk), lambda b, i: (b, i)),
            pl.BlockSpec((t, dk), lambda b, i: (b, h + i)),
            pl.BlockSpec((t, dv), lambda b, i: (b, h + i)),
            pl.BlockSpec((t, dv), lambda b, i: (b, 2 * h + i)),
            pl.BlockSpec((t, 2 * GLA_RANK), lambda b, i: (b, 0)),
            pl.BlockSpec((None, 2, GLA_RANK, dk), lambda b, i: (layer, 0, 0, i)),
            pl.BlockSpec((None, 2, 1, dk), lambda b, i: (layer, 0, 0, i)),
            pl.BlockSpec((None, 1, dv), lambda b, i: (layer, 0, 0)),
        ],
        out_specs=pl.BlockSpec((t, dv), lambda b, i: (b, i)),
        out_shape=jax.ShapeDtypeStruct((m, GLA_DV), BF16),
        scratch_shapes=[pltpu.VMEM((2, t, dk), F32),
                        pltpu.VMEM((2, t, dk), BF16), pltpu.VMEM((2, t, dk), BF16),
                        pltpu.VMEM((2, t // (GLA_CHUNK * GLA_GROUP), dk), F32),
                        pltpu.VMEM((t, dv), F32), pltpu.VMEM((t, dv), F32),
                        pltpu.VMEM((dv, dk), F32), pltpu.VMEM((dv, dk), F32)],
        compiler_params=_params("parallel", "arbitrary"),
        name="gla_core",
    )(p, p, p, p, lr, w_gate, b_gate, norm_g)


NA_DR = 2 * NA_KR
NA_DC = 2 * NA_KC - 1
NA_PAIR_W = 2 * GRID_W


def _na_col_geometry(shape):
    log_w = GRID_W.bit_length() - 1
    assert 1 << log_w == GRID_W
    col = lax.broadcasted_iota(jnp.int32, shape, 1)
    qc = col >> (log_w + 1)
    half = (col >> log_w) & 1
    kc = col & (GRID_W - 1)
    win = jnp.clip(qc - NA_KC // 2, 0, GRID_W - NA_KC)
    valid = (kc >= win) & (kc < win + NA_KC)
    src = half * NA_DC + (kc - qc + NA_KC - 1)
    return src, valid


def _na_bias_kernel(rpb2_ref, o_ref, sel_ref, valid_ref):
    n = o_ref.shape[1]

    @pl.when((pl.program_id(0) == 0) & (pl.program_id(1) == 0))
    def _build_selector():
        src, valid = _na_col_geometry((GRID_W, n))
        r = lax.broadcasted_iota(jnp.int32, (GRID_W, n), 0)
        sel_ref[...] = jnp.where(valid & (r == src), 1.0, 0.0).astype(BF16)
        _, valid_o = _na_col_geometry((NA_DR, n))
        valid_ref[...] = jnp.where(valid_o, 1.0, 0.0)

    sel = sel_ref[...]
    acc = jnp.zeros(o_ref.shape, F32)
    for term in _bf16_terms(rpb2_ref[...], 3):
        acc = acc + jnp.dot(term, sel, preferred_element_type=F32)
    o_ref[...] = jnp.where(valid_ref[...] > 0.5, acc, NEG_INF) * LOG2E


def _na_bias_tiles(rpb):
    nl, nh = rpb.shape[:2]
    assert rpb.shape[2] == NA_DR - 1 and rpb.shape[3] == NA_DC and 2 * NA_DC <= GRID_W
    rp = jnp.pad(rpb.astype(F32), ((0, 0), (0, 0), (0, 2), (0, 0)))
    pad = jnp.zeros((nl, nh, NA_DR, GRID_W - 2 * NA_DC), F32)
    rpb2 = jnp.concatenate([rp[:, :, :NA_DR], rp[:, :, 1:NA_DR + 1], pad], axis=-1)
    n = GRID_W * NA_PAIR_W
    tiles = pl.pallas_call(
        _na_bias_kernel,
        grid=(nl, nh),
        in_specs=[pl.BlockSpec((None, None, NA_DR, GRID_W), lambda l, h: (l, h, 0, 0))],
        out_specs=pl.BlockSpec((None, None, NA_DR, n), lambda l, h: (l, h, 0, 0)),
        out_shape=jax.ShapeDtypeStruct((nl, nh, NA_DR, n), F32),
        scratch_shapes=[pltpu.VMEM((GRID_W, n), BF16), pltpu.VMEM((NA_DR, n), F32)],
        compiler_params=_params("arbitrary", "arbitrary"),
        name="na_bias",
    )(rpb2)
    return tiles.reshape(nl, nh, NA_DR, GRID_W, NA_PAIR_W)


def _na_kernel(q_ref, k_ref, v_ref, bias_ref, o_ref):
    rows = q_ref.shape[0] // GRID_W
    band = NA_KR * GRID_W
    scale = NA_HD ** -0.5 * LOG2E

    def head_body(hh, it):
        hs = slice(hh * NA_HD, (hh + 1) * NA_HD)
        group = []
        for u in range(NA_ROW_GROUP):
            r = it * NA_ROW_GROUP + u
            if isinstance(r, int):
                rs = min(max(r - NA_KR // 2, 0), rows - NA_KR)
                qs, ks = pl.ds(r * GRID_W, GRID_W), pl.ds(rs * GRID_W, band)
            else:
                rs = jnp.clip(r - NA_KR // 2, 0, rows - NA_KR)
                qs = pl.ds(pl.multiple_of(r * GRID_W, GRID_W), GRID_W)
                ks = pl.ds(pl.multiple_of(rs * GRID_W, GRID_W), band)
            d0 = rs - r + NA_KR - 1
            s = lax.dot_general(q_ref[qs, hs], k_ref[ks, hs], (((1,), (1,)), ((), ())),
                                preferred_element_type=F32)
            group.append((qs, ks, d0, s))
        probs = []
        for qs, ks, d0, s in group:
            bias = jnp.concatenate([bias_ref[hh, d0 + 2 * p] for p in range(NA_KR // 2)],
                                   axis=1)
            s = s * scale + bias
            p = jnp.exp2(s - jnp.max(s, axis=-1, keepdims=True))
            l = jnp.sum(p, axis=-1, keepdims=True)
            probs.append((qs, ks, p.astype(BF16), l))
        for qs, ks, p, l in probs:
            o = jnp.dot(p, v_ref[ks, hs], preferred_element_type=F32)
            o_ref[qs, hs] = (o / l).astype(o_ref.dtype)

    n_groups = rows // NA_ROW_GROUP
    for hh in range(NA_HEADS_PER_STEP):
        if n_groups == 1:
            head_body(hh, 0)
        else:
            def body(it, carry, hh=hh):
                head_body(hh, it)
                return carry

            lax.fori_loop(0, n_groups, body, 0)


def _na_core(qkv, bias, layer, batch):
    m = qkv.shape[0]
    t = m // batch
    hps = NA_HEADS_PER_STEP
    assert NA_HEADS % hps == 0
    nb, w = NA_HEADS // hps, hps * NA_HD
    return pl.pallas_call(
        _na_kernel,
        grid=(batch, nb),
        in_specs=[
            pl.BlockSpec((t, w), lambda b, i: (b, i)),
            pl.BlockSpec((t, w), lambda b, i: (b, nb + i)),
            pl.BlockSpec((t, w), lambda b, i: (b, 2 * nb + i)),
            pl.BlockSpec((None, hps, NA_DR, GRID_W, NA_PAIR_W),
                         lambda b, i: (layer, i, 0, 0, 0)),
        ],
        out_specs=pl.BlockSpec((t, w), lambda b, i: (b, i)),
        out_shape=jax.ShapeDtypeStruct((m, D_MODEL), BF16),
        compiler_params=_params("parallel", "arbitrary"),
        name="na_core",
    )(qkv, qkv, qkv, bias)


def kernel(x, gla_w_in, gla_w_gate_up, gla_b_gate, gla_norm_g, gla_w_out, na_w_in, na_rpb, na_w_out, ffn_w_up, ffn_conv_w, ffn_conv_b, ffn_w_down, ln_mix_g, ln_mix_b, ln_ffn_g, ln_ffn_b):
    batch, seq, d = x.shape
    assert seq == SEQ and d == D_MODEL
    n_main = 2 * GLA_DK + 2 * GLA_DV
    gla_w = gla_w_in.astype(BF16)
    gla_w_lr = gla_w_in[:, :, n_main:].astype(BF16)
    gla_bg = gla_b_gate.reshape(gla_b_gate.shape[0], 2, 1, GLA_DK)
    gla_ng = gla_norm_g.reshape(gla_norm_g.shape[0], 1, GLA_DV_HEAD)
    na_w = na_w_in.astype(BF16)
    na_bias = _na_bias_tiles(na_rpb)
    w_up = ffn_w_up.astype(BF16)
    conv_b = ffn_conv_b.reshape(DEPTH, 1, 2 * D_FF)
    mix_g, mix_b = ln_mix_g.reshape(DEPTH, 1, d), ln_mix_b.reshape(DEPTH, 1, d)
    ffn_g, ffn_b = ln_ffn_g.reshape(DEPTH, 1, d), ln_ffn_b.reshape(DEPTH, 1, d)

    xf = x.reshape(batch * seq, d)
    for i in range(DEPTH):
        j = i // 2
        if i % 2 == 0:
            p, lr = _matmul(xf, gla_w, j, n_main, BF16, tm=PROJ_TM, tn=PROJ_TN, w_side=gla_w_lr)
            mix = _gla_core(p, lr, gla_w_gate_up, gla_bg, gla_ng, j, batch)
            w_out = gla_w_out
        else:
            qkv = _matmul(xf, na_w, j, 3 * d, BF16, tm=PROJ_TM, tn=PROJ_TN)
            mix = _na_core(qkv, na_bias, j, batch)
            w_out = na_w_out
        xf = _matmul_res_ln(mix, w_out, j, xf, mix_g, mix_b, i, tm=OUT_TM)
        xf = _conv_ffn_ln(xf, w_up, ffn_conv_w, conv_b, ffn_w_down, ffn_g, ffn_b, i,
                          tm=FFN_TM, tf=FFN_TF)
    return xf.reshape(batch, seq, d)
```

```python
import functools
import math

import jax
import jax.numpy as jnp
import numpy as np
from jax import lax
from jax.experimental import pallas as pl
from jax.experimental.pallas import tpu as pltpu

F32 = jnp.float32
BF16 = jnp.bfloat16

D_MODEL = 2048
SEQ = 2048
DEPTH = 4
GRID_W = 64
GLA_HEADS = 4
GLA_DK = D_MODEL // 2
GLA_DV = D_MODEL
GLA_DK_HEAD = GLA_DK // GLA_HEADS
GLA_DV_HEAD = GLA_DV // GLA_HEADS
GLA_RANK = 16
GLA_TEMP = 16.0
GLA_CHUNK = 64
GLA_GROUP = 4
GLA_LOCAL_GROUPS = 8
GLA_STATE_GROUP = 8
NA_HEADS = 16
NA_HD = D_MODEL // NA_HEADS
NA_KR = 8
NA_KC = 16
NA_HEADS_PER_STEP = 1
NA_ROW_GROUP = 32
D_FF = 11 * D_MODEL // 4
LN_EPS = 1e-5
RMS_EPS = 1e-6
NEG_INF = -1e9
ALPHA = (2.0 * DEPTH) ** 0.25
LOG2E = math.log2(math.e)

V7X_VMEM_BYTES = 64 * 1024 * 1024
VMEM_LIMIT = V7X_VMEM_BYTES - 8 * 1024 * 1024
LN_ROW_SPLIT = 4
OUT_TM = 512
PROJ_TM = 1024
PROJ_TN = 2048
FFN_TM = 1024
FFN_TF = 512
FFN_SUB = 512
HALO = 8


def _params(*sem):
    return pltpu.CompilerParams(dimension_semantics=sem, vmem_limit_bytes=VMEM_LIMIT)


def _layer_norm(y, g, b):
    mu = jnp.mean(y, axis=-1, keepdims=True)
    yc = y - mu
    var = jnp.mean(yc * yc, axis=-1, keepdims=True)
    return yc * lax.rsqrt(var + LN_EPS) * g + b


def _mm_kernel(x_ref, w_ref, o_ref):
    x = x_ref[...].astype(BF16)
    o_ref[...] = jnp.dot(x, w_ref[...], preferred_element_type=F32).astype(o_ref.dtype)


def _mm_side_kernel(x_ref, w_ref, ws_ref, o_ref, os_ref):
    x = x_ref[...].astype(BF16)
    o_ref[...] = jnp.dot(x, w_ref[...], preferred_element_type=F32).astype(o_ref.dtype)

    @pl.when(pl.program_id(1) == 0)
    def _side():
        os_ref[...] = jnp.dot(x, ws_ref[...], preferred_element_type=F32)


def _matmul(x, w, layer, n, out_dtype, tm, tn, w_side=None):
    m, k = x.shape
    assert m % tm == 0 and n % tn == 0 and n <= w.shape[2]
    in_specs = [pl.BlockSpec((tm, k), lambda i, j: (i, 0)),
                pl.BlockSpec((None, k, tn), lambda i, j: (layer, 0, j))]
    out_specs = pl.BlockSpec((tm, tn), lambda i, j: (i, j))
    out_shape = jax.ShapeDtypeStruct((m, n), out_dtype)
    if w_side is None:
        body, operands = _mm_kernel, (x, w)
    else:
        s = w_side.shape[2]
        body, operands = _mm_side_kernel, (x, w, w_side)
        in_specs.append(pl.BlockSpec((None, k, s), lambda i, j: (layer, 0, 0)))
        out_specs = (out_specs, pl.BlockSpec((tm, s), lambda i, j: (i, 0)))
        out_shape = (out_shape, jax.ShapeDtypeStruct((m, s), F32))
    return pl.pallas_call(
        body,
        grid=(m // tm, n // tn),
        in_specs=in_specs,
        out_specs=out_specs,
        out_shape=out_shape,
        compiler_params=_params("parallel", "arbitrary"),
        name="in_proj",
    )(*operands)


def _mm_ln_kernel(a_ref, w_ref, x_ref, g_ref, b_ref, o_ref, wbf_ref):
    @pl.when(pl.program_id(0) == 0)
    def _cast_weights():
        wbf_ref[...] = w_ref[...].astype(BF16)

    sub = a_ref.shape[0] // LN_ROW_SPLIT
    for r in range(LN_ROW_SPLIT):
        rs = slice(r * sub, (r + 1) * sub)
        m = jnp.dot(a_ref[rs, :], wbf_ref[...], preferred_element_type=F32)
        y = ALPHA * x_ref[rs, :] + m
        o_ref[rs, :] = _layer_norm(y, g_ref[...], b_ref[...])


def _matmul_res_ln(a, w, w_layer, x, g, b, ln_layer, tm):
    m, k = a.shape
    d = w.shape[2]
    return pl.pallas_call(
        _mm_ln_kernel,
        grid=(m // tm,),
        in_specs=[pl.BlockSpec((tm, k), lambda i: (i, 0)),
                  pl.BlockSpec((None, k, d), lambda i: (w_layer, 0, 0),
                               pipeline_mode=pl.Buffered(1)),
                  pl.BlockSpec((tm, d), lambda i: (i, 0)),
                  pl.BlockSpec((None, 1, d), lambda i: (ln_layer, 0, 0)),
                  pl.BlockSpec((None, 1, d), lambda i: (ln_layer, 0, 0))],
        out_specs=pl.BlockSpec((tm, d), lambda i: (i, 0)),
        out_shape=jax.ShapeDtypeStruct((m, d), F32),
        scratch_shapes=[pltpu.VMEM((k, d), BF16)],
        compiler_params=_params("arbitrary"),
        name="out_proj_ln",
    )(a, w, x, g, b)


def _gelu(a):
    return 0.5 * a * (1.0 + lax.erf(a * (2.0 ** -0.5)))


def _ffn_kernel(xp_ref, x_hbm, xn_ref, wa_ref, wb_ref, cw_ref, cb_ref, wd_ref, g_ref, b_ref,
                o_ref, xe_ref, xbuf_ref, x_sem, *, tm, tiles_per_seq):
    i = pl.program_id(0)
    j = pl.program_id(1)
    ni = pl.num_programs(0)
    nj = pl.num_programs(1)
    rows = tm + 2 * HALO
    tf = wd_ref.shape[0]
    d_ff = cw_ref.shape[1] // 2

    def x_copy(tile):
        return pltpu.make_async_copy(x_hbm.at[pl.ds(tile * tm, tm), :], xbuf_ref, x_sem)

    @pl.when(j == 0)
    def _stage_rows():
        @pl.when(i == 0)
        def _first_tile():
            x_copy(0).start()

        x_copy(i).wait()
        pos = i % tiles_per_seq
        keep_prev = (pos != 0).astype(F32)
        keep_next = (pos != tiles_per_seq - 1).astype(F32)
        staged = jnp.concatenate([xp_ref[...] * keep_prev, xbuf_ref[...], xn_ref[...] * keep_next],
                                 axis=0)
        xe_ref[...] = staged.astype(BF16)
        o_ref[...] = ALPHA * xbuf_ref[...]

    @pl.when((j == 1) & (i + 1 < ni))
    def _prefetch_next_tile():
        x_copy(i + 1).start()

    xe = xe_ref[...]

    def up(w_ref, col):
        return jnp.dot(xe, w_ref[:, col:col + FFN_SUB], preferred_element_type=F32)

    def conv(h, p_col):
        h_prev = pltpu.roll(h, 1, 0)[HALO:HALO + tm]
        h_next = pltpu.roll(h, rows - 1, 0)[HALO:HALO + tm]
        ps = pl.ds(pl.multiple_of(p_col, FFN_SUB), FFN_SUB)
        cw = cw_ref[:, ps]
        return (h_prev * cw[0:1] + h[HALO:HALO + tm] * cw[1:2] + h_next * cw[2:3]
                + cb_ref[:, ps])

    contrib = None
    for c in range(tf // FFN_SUB):
        col = c * FFN_SUB
        a = conv(up(wa_ref, col), j * tf + col)
        bb = conv(up(wb_ref, col), d_ff + j * tf + col)
        u = (_gelu(a) * bb).astype(BF16)
        wd = wd_ref[col:col + FFN_SUB, :].astype(BF16)
        part = jnp.dot(u, wd, preferred_element_type=F32)
        contrib = part if contrib is None else contrib + part
    o_ref[...] += contrib

    @pl.when(j == nj - 1)
    def _finish():
        o_ref[...] = _layer_norm(o_ref[...], g_ref[...], b_ref[...])


def _conv_ffn_ln(x, w_up, conv_w, conv_b, w_down, g, b, layer, tm, tf):
    m, d = x.shape
    f = w_down.shape[1]
    nf = f // tf
    assert m % tm == 0 and f % tf == 0 and SEQ % tm == 0 and tm % 8 == 0
    assert tf % FFN_SUB == 0 and nf >= 2
    tb = tm // 8
    last_blk = m // 8 - 1
    kern = functools.partial(_ffn_kernel, tm=tm, tiles_per_seq=SEQ // tm)
    return pl.pallas_call(
        kern,
        grid=(m // tm, nf),
        in_specs=[
            pl.BlockSpec((8, d), lambda i, j: (jnp.maximum(i * tb - 1, 0), 0)),
            pl.BlockSpec(memory_space=pl.ANY),
            pl.BlockSpec((8, d), lambda i, j: (jnp.minimum((i + 1) * tb, last_blk), 0)),
            pl.BlockSpec((None, d, tf), lambda i, j: (layer, 0, j)),
            pl.BlockSpec((None, d, tf), lambda i, j: (layer, 0, j + nf)),
            pl.BlockSpec((None, 3, 2 * f), lambda i, j: (layer, 0, 0)),
            pl.BlockSpec((None, 1, 2 * f), lambda i, j: (layer, 0, 0)),
            pl.BlockSpec((None, tf, d), lambda i, j: (layer, j, 0)),
            pl.BlockSpec((None, 1, d), lambda i, j: (layer, 0, 0)),
            pl.BlockSpec((None, 1, d), lambda i, j: (layer, 0, 0)),
        ],
        out_specs=pl.BlockSpec((tm, d), lambda i, j: (i, 0), pipeline_mode=pl.Buffered(1)),
        out_shape=jax.ShapeDtypeStruct((m, d), F32),
        scratch_shapes=[pltpu.VMEM((tm + 2 * HALO, d), BF16),
                        pltpu.VMEM((tm, d), F32),
                        pltpu.SemaphoreType.DMA(())],
        compiler_params=_params("arbitrary", "arbitrary"),
        name="conv_ffn_ln",
    )(x, x, x, w_up, w_up, conv_w, conv_b, w_down, g, b)


def _bf16_terms(x, n_terms):
    terms = []
    for _ in range(n_terms - 1):
        t = x.astype(BF16)
        terms.append(t)
        x = x - t.astype(F32)
    terms.append(x.astype(BF16))
    return terms


def _split_dot(tri, x):
    t = tri.astype(BF16)
    hi, lo = _bf16_terms(x, 2)
    return (jnp.dot(t, hi, preferred_element_type=F32)
            + jnp.dot(t, lo, preferred_element_type=F32))


def _log_sigmoid(x):
    return jnp.minimum(x, 0.0) - jnp.log(1.0 + jnp.exp(-jnp.abs(x)))


def _gla_kernel(q_ref, k_ref, v_ref, g_ref, lr_ref, wg_ref, bg_ref, ng_ref, o_ref,
                la_ref, qg_ref, kg_ref, dg_ref, of_ref, ob_ref, sf_ref, sb_ref):
    t = q_ref.shape[0]
    c = GLA_CHUNK
    gp = GLA_GROUP
    gr = gp * c
    n_groups = t // gr
    scale = GLA_DK_HEAD ** -0.5
    nt = (((1,), (1,)), ((), ()))
    tn = (((0,), (0,)), ((), ()))

    lr = lr_ref[...]
    for z in range(2):
        logit = jnp.dot(lr[:, z * GLA_RANK:(z + 1) * GLA_RANK].astype(BF16),
                        wg_ref[z].astype(BF16), preferred_element_type=F32)
        la_ref[z] = _log_sigmoid(logit + bg_ref[z]) * (LOG2E / GLA_TEMP)

    row = lax.broadcasted_iota(jnp.int32, (c, c), 0)
    col = lax.broadcasted_iota(jnp.int32, (c, c), 1)
    tri = ((row >= col).astype(F32), (row <= col).astype(F32))
    tot_row = (c - 1, 0)
    qrow = lax.broadcasted_iota(jnp.int32, (c, gr), 0)
    klane = lax.broadcasted_iota(jnp.int32, (c, gr), 1)
    s_refs = (sf_ref, sb_ref)
    o_refs = (of_ref, ob_ref)

    def sum_tots(tots, lo, hi):
        acc = jnp.zeros_like(tots[0])
        for p in range(lo, hi):
            acc = acc + tots[p]
        return acc

    def chunk_slices(g):
        return [pl.ds(pl.multiple_of(g * gr + p * c, c), c) for p in range(gp)]

    def local_cums(z, g):
        return [_split_dot(tri[z], la_ref[z, sl, :]) for sl in chunk_slices(g)]

    def local_scores(z, g, cums):
        sls = chunk_slices(g)
        tots = [cum[tot_row[z]:tot_row[z] + 1, :] for cum in cums]
        qd, ki, ks = [], [], []
        for sl, cum, tot in zip(sls, cums, tots):
            q = q_ref[sl, :].astype(F32) * scale
            k = k_ref[sl, :].astype(F32)
            qd.append(q * jnp.exp2(cum))
            ki.append((k * jnp.exp2(-cum)).astype(BF16))
            ks.append(k * jnp.exp2(tot - cum))
        if z == 0:
            before = [sum_tots(tots, 0, p) for p in range(gp)]
            after = [sum_tots(tots, p + 1, gp) for p in range(gp)]
        else:
            before = [sum_tots(tots, p + 1, gp) for p in range(gp)]
            after = [sum_tots(tots, 0, p) for p in range(gp)]
        for p in range(gp):
            qg_ref[z, sls[p], :] = (qd[p] * jnp.exp2(before[p])).astype(BF16)
            kg_ref[z, sls[p], :] = (ks[p] * jnp.exp2(after[p])).astype(BF16)
        dg_ref[z, pl.ds(g, 1), :] = jnp.exp2(sum_tots(tots, 0, gp))
        scores = []
        for pq in range(gp):
            keys = []
            for pk in range(gp):
                seen = pk < pq if z == 0 else pk > pq
                if pk == pq:
                    keys.append(ki[pk])
                elif not seen:
                    keys.append(jnp.zeros_like(ki[pk]))
                else:
                    lo, hi = (pk + 1, pq) if z == 0 else (pq + 1, pk)
                    keys.append((ks[pk] * jnp.exp2(sum_tots(tots, lo, hi))).astype(BF16))
            scores.append(lax.dot_general(qd[pq].astype(BF16), jnp.concatenate(keys, axis=0),
                                          nt, preferred_element_type=F32))
        return scores

    def local_out(z, g, scores):
        probs = []
        for pq, s in enumerate(scores):
            visible = (klane <= qrow + pq * c) if z == 0 else (klane > qrow + pq * c)
            probs.append(jnp.where(visible, s, 0.0).astype(BF16))
        rows = pl.ds(pl.multiple_of(g * gr, gr), gr)
        o_refs[z][rows, :] = jnp.dot(jnp.concatenate(probs, axis=0), v_ref[rows, :],
                                     preferred_element_type=F32)

    def local_body(it, carry):
        work = [(z, it * GLA_LOCAL_GROUPS + u) for u in range(GLA_LOCAL_GROUPS) for z in range(2)]
        cums = [local_cums(z, g) for z, g in work]
        scores = [local_scores(z, g, cum) for (z, g), cum in zip(work, cums)]
        for (z, g), s in zip(work, scores):
            local_out(z, g, s)
        return carry

    lax.fori_loop(0, n_groups // GLA_LOCAL_GROUPS, local_body, 0)

    sf_ref[...] = jnp.zeros_like(sf_ref)
    sb_ref[...] = jnp.zeros_like(sb_ref)

    def state_body(it, carry):
        steps = []
        for z in range(2):
            for u in range(GLA_STATE_GROUP):
                i = it * GLA_STATE_GROUP + u
                g = i if z == 0 else n_groups - 1 - i
                rows = pl.ds(pl.multiple_of(g * gr, gr), gr)
                upd = lax.dot_general(v_ref[rows, :], kg_ref[z, rows, :], tn,
                                      preferred_element_type=F32)
                steps.append((z, g, rows, upd))
        st = [sf_ref[...], sb_ref[...]]
        for z, g, rows, upd in steps:
            o_refs[z][rows, :] += lax.dot_general(qg_ref[z, rows, :], st[z].astype(BF16), nt,
                                                  preferred_element_type=F32)
            st[z] = st[z] * dg_ref[z, pl.ds(g, 1), :] + upd
        sf_ref[...] = st[0]
        sb_ref[...] = st[1]
        return carry

    lax.fori_loop(0, n_groups // GLA_STATE_GROUP, state_body, 0)

    o = of_ref[...] + ob_ref[...]
    o = o * lax.rsqrt(jnp.mean(o * o, axis=-1, keepdims=True) + RMS_EPS) * ng_ref[...]
    gate = g_ref[...].astype(F32)
    o_ref[...] = (o * (gate * jax.nn.sigmoid(gate))).astype(o_ref.dtype)


def _gla_core(p, lr, w_gate, b_gate, norm_g, layer, batch):
    m = p.shape[0]
    t = m // batch
    h, dk, dv = GLA_HEADS, GLA_DK_HEAD, GLA_DV_HEAD
    return pl.pallas_call(
        _gla_kernel,
        grid=(batch, h),
        in_specs=[
            pl.BlockSpec((t, dk), lambda b, i: (b, i)),
            pl.BlockSpec((t, dk), lambda b, i: (b, h + i)),
            pl.BlockSpec((t, dv), lambda b, i: (b, h + i)),
            pl.BlockSpec((t, dv), lambda b, i: (b, 2 * h + i)),
            pl.BlockSpec((t, 2 * GLA_RANK), lambda b, i: (b, 0)),
            pl.BlockSpec((None, 2, GLA_RANK, dk), lambda b, i: (layer, 0, 0, i)),
            pl.BlockSpec((None, 2, 1, dk), lambda b, i: (layer, 0, 0, i)),
            pl.BlockSpec((None, 1, dv), lambda b, i: (layer, 0, 0)),
        ],
        out_specs=pl.BlockSpec((t, dv), lambda b, i: (b, i)),
        out_shape=jax.ShapeDtypeStruct((m, GLA_DV), BF16),
        scratch_shapes=[pltpu.VMEM((2, t, dk), F32),
                        pltpu.VMEM((2, t, dk), BF16), pltpu.VMEM((2, t, dk), BF16),
                        pltpu.VMEM((2, t // (GLA_CHUNK * GLA_GROUP), dk), F32),
                        pltpu.VMEM((t, dv), F32), pltpu.VMEM((t, dv), F32),
                        pltpu.VMEM((dv, dk), F32), pltpu.VMEM((dv, dk), F32)],
        compiler_params=_params("parallel", "arbitrary"),
        name="gla_core",
    )(p, p, p, p, lr, w_gate, b_gate, norm_g)


NA_DR = 2 * NA_KR
NA_DC = 2 * NA_KC - 1
NA_PAIR_W = 2 * GRID_W


def _na_col_geometry(shape):
    log_w = GRID_W.bit_length() - 1
    assert 1 << log_w == GRID_W
    col = lax.broadcasted_iota(jnp.int32, shape, 1)
    qc = col >> (log_w + 1)
    half = (col >> log_w) & 1
    kc = col & (GRID_W - 1)
    win = jnp.clip(qc - NA_KC // 2, 0, GRID_W - NA_KC)
    valid = (kc >= win) & (kc < win + NA_KC)
    src = half * NA_DC + (kc - qc + NA_KC - 1)
    return src, valid


def _na_bias_kernel(rpb2_ref, o_ref, sel_ref, valid_ref):
    n = o_ref.shape[1]

    @pl.when((pl.program_id(0) == 0) & (pl.program_id(1) == 0))
    def _build_selector():
        src, valid = _na_col_geometry((GRID_W, n))
        r = lax.broadcasted_iota(jnp.int32, (GRID_W, n), 0)
        sel_ref[...] = jnp.where(valid & (r == src), 1.0, 0.0).astype(BF16)
        _, valid_o = _na_col_geometry((NA_DR, n))
        valid_ref[...] = jnp.where(valid_o, 1.0, 0.0)

    sel = sel_ref[...]
    acc = jnp.zeros(o_ref.shape, F32)
    for term in _bf16_terms(rpb2_ref[...], 3):
        acc = acc + jnp.dot(term, sel, preferred_element_type=F32)
    o_ref[...] = jnp.where(valid_ref[...] > 0.5, acc, NEG_INF) * LOG2E


def _na_bias_tiles(rpb):
    nl, nh = rpb.shape[:2]
    assert rpb.shape[2] == NA_DR - 1 and rpb.shape[3] == NA_DC and 2 * NA_DC <= GRID_W
    rp = jnp.pad(rpb.astype(F32), ((0, 0), (0, 0), (0, 2), (0, 0)))
    pad = jnp.zeros((nl, nh, NA_DR, GRID_W - 2 * NA_DC), F32)
    rpb2 = jnp.concatenate([rp[:, :, :NA_DR], rp[:, :, 1:NA_DR + 1], pad], axis=-1)
    n = GRID_W * NA_PAIR_W
    tiles = pl.pallas_call(
        _na_bias_kernel,
        grid=(nl, nh),
        in_specs=[pl.BlockSpec((None, None, NA_DR, GRID_W), lambda l, h: (l, h, 0, 0))],
        out_specs=pl.BlockSpec((None, None, NA_DR, n), lambda l, h: (l, h, 0, 0)),
        out_shape=jax.ShapeDtypeStruct((nl, nh, NA_DR, n), F32),
        scratch_shapes=[pltpu.VMEM((GRID_W, n), BF16), pltpu.VMEM((NA_DR, n), F32)],
        compiler_params=_params("arbitrary", "arbitrary"),
        name="na_bias",
    )(rpb2)
    return tiles.reshape(nl, nh, NA_DR, GRID_W, NA_PAIR_W)


def _na_kernel(q_ref, k_ref, v_ref, bias_ref, o_ref):
    rows = q_ref.shape[0] // GRID_W
    band = NA_KR * GRID_W
    scale = NA_HD ** -0.5 * LOG2E

    def head_body(hh, it):
        hs = slice(hh * NA_HD, (hh + 1) * NA_HD)
        group = []
        for u in range(NA_ROW_GROUP):
            r = it * NA_ROW_GROUP + u
            if isinstance(r, int):
                rs = min(max(r - NA_KR // 2, 0), rows - NA_KR)
                qs, ks = pl.ds(r * GRID_W, GRID_W), pl.ds(rs * GRID_W, band)
            else:
                rs = jnp.clip(r - NA_KR // 2, 0, rows - NA_KR)
                qs = pl.ds(pl.multiple_of(r * GRID_W, GRID_W), GRID_W)
                ks = pl.ds(pl.multiple_of(rs * GRID_W, GRID_W), band)
            d0 = rs - r + NA_KR - 1
            s = lax.dot_general(q_ref[qs, hs], k_ref[ks, hs], (((1,), (1,)), ((), ())),
                                preferred_element_type=F32)
            group.append((qs, ks, d0, s))
        probs = []
        for qs, ks, d0, s in group:
            bias = jnp.concatenate([bias_ref[hh, d0 + 2 * p] for p in range(NA_KR // 2)],
                                   axis=1)
            s = s * scale + bias
            p = jnp.exp2(s - jnp.max(s, axis=-1, keepdims=True))
            l = jnp.sum(p, axis=-1, keepdims=True)
            probs.append((qs, ks, p.astype(BF16), l))
        for qs, ks, p, l in probs:
            o = jnp.dot(p, v_ref[ks, hs], preferred_element_type=F32)
            o_ref[qs, hs] = (o / l).astype(o_ref.dtype)

    n_groups = rows // NA_ROW_GROUP
    for hh in range(NA_HEADS_PER_STEP):
        if n_groups == 1:
            head_body(hh, 0)
        else:
            def body(it, carry, hh=hh):
                head_body(hh, it)
                return carry

            lax.fori_loop(0, n_groups, body, 0)


def _na_core(qkv, bias, layer, batch):
    m = qkv.shape[0]
    t = m // batch
    hps = NA_HEADS_PER_STEP
    assert NA_HEADS % hps == 0
    nb, w = NA_HEADS // hps, hps * NA_HD
    return pl.pallas_call(
        _na_kernel,
        grid=(batch, nb),
        in_specs=[
            pl.BlockSpec((t, w), lambda b, i: (b, i)),
            pl.BlockSpec((t, w), lambda b, i: (b, nb + i)),
            pl.BlockSpec((t, w), lambda b, i: (b, 2 * nb + i)),
            pl.BlockSpec((None, hps, NA_DR, GRID_W, NA_PAIR_W),
                         lambda b, i: (layer, i, 0, 0, 0)),
        ],
        out_specs=pl.BlockSpec((t, w), lambda b, i: (b, i)),
        out_shape=jax.ShapeDtypeStruct((m, D_MODEL), BF16),
        compiler_params=_params("parallel", "arbitrary"),
        name="na_core",
    )(qkv, qkv, qkv, bias)


def kernel(x, gla_w_in, gla_w_gate_up, gla_b_gate, gla_norm_g, gla_w_out, na_w_in, na_rpb, na_w_out, ffn_w_up, ffn_conv_w, ffn_conv_b, ffn_w_down, ln_mix_g, ln_mix_b, ln_ffn_g, ln_ffn_b):
    batch, seq, d = x.shape
    assert seq == SEQ and d == D_MODEL
    n_main = 2 * GLA_DK + 2 * GLA_DV
    gla_w = gla_w_in.astype(BF16)
    gla_w_lr = gla_w_in[:, :, n_main:].astype(BF16)
    gla_bg = gla_b_gate.reshape(gla_b_gate.shape[0], 2, 1, GLA_DK)
    gla_ng = gla_norm_g.reshape(gla_norm_g.shape[0], 1, GLA_DV_HEAD)
    na_w = na_w_in.astype(BF16)
    na_bias = _na_bias_tiles(na_rpb)
    w_up = ffn_w_up.astype(BF16)
    conv_b = ffn_conv_b.reshape(DEPTH, 1, 2 * D_FF)
    mix_g, mix_b = ln_mix_g.reshape(DEPTH, 1, d), ln_mix_b.reshape(DEPTH, 1, d)
    ffn_g, ffn_b = ln_ffn_g.reshape(DEPTH, 1, d), ln_ffn_b.reshape(DEPTH, 1, d)

    xf = x.reshape(batch * seq, d)
    for i in range(DEPTH):
        j = i // 2
        if i % 2 == 0:
            p, lr = _matmul(xf, gla_w, j, n_main, BF16, tm=PROJ_TM, tn=PROJ_TN, w_side=gla_w_lr)
            mix = _gla_core(p, lr, gla_w_gate_up, gla_bg, gla_ng, j, batch)
            w_out = gla_w_out
        else:
            qkv = _matmul(xf, na_w, j, 3 * d, BF16, tm=PROJ_TM, tn=PROJ_TN)
            mix = _na_core(qkv, na_bias, j, batch)
            w_out = na_w_out
        xf = _matmul_res_ln(mix, w_out, j, xf, mix_g, mix_b, i, tm=OUT_TM)
        xf = _conv_ffn_ln(xf, w_up, ffn_conv_w, conv_b, ffn_w_down, ffn_g, ffn_b, i,
                          tm=FFN_TM, tf=FFN_TF)
    return xf.reshape(batch, seq, d)
```
